```python
import math, functools
import jax, jax.numpy as jnp
from jax import lax
import numpy as np

D_MODEL = 1024
BATCH = 8
SEQ = 4096
DEPTH = 1
DEC_BATCH = 32
DEC_SEQ = 1
PAST_LEN = 16384
PAGE_SIZE = 128

H_FOX = 16
HD_FOX = 64
W_FOX = H_FOX * HD_FOX
Q_BLOCK = 128
FOX_BIAS_LO = 3.0
FOX_BIAS_HI = 12.0
H_GLA = 4
DK_GLA = D_MODEL // 2 // H_GLA
DV_GLA = D_MODEL // H_GLA
W_GLA_K = H_GLA * DK_GLA
W_GLA_V = H_GLA * DV_GLA
GLA_RANK = 16
GLA_NORMALIZER = 16.0
GLA_CHUNK = 64
MEM_LEN = 256
H_MEM = 4
HD_MEM = 64
W_MEM = H_MEM * HD_MEM
N_BRANCH = 3
D_FF = 4 * D_MODEL
EPS = 1e-6

SPLITS = (W_FOX, W_FOX, W_FOX, H_FOX, W_GLA_K, W_GLA_K, W_GLA_V, GLA_RANK, W_GLA_V, W_MEM, N_BRANCH * D_MODEL)
D_IN = sum(SPLITS)
SPLIT_POINTS = tuple(int(s) for s in np.cumsum(SPLITS)[:-1])

kernel_name = 'fox_gla_memory_gated_hybrid_step'


def rmsnorm(x, g):
    xf = x.astype(jnp.float32)
    y = xf * lax.rsqrt(jnp.mean(xf * xf, axis=-1, keepdims=True) + EPS)
    return (y * g.astype(jnp.float32)).astype(x.dtype)


def fox_prompt(q, k, v, logf):
    B, S, H, D = q.shape
    nb = S // Q_BLOCK
    scale = HD_FOX ** -0.5
    ct = jnp.cumsum(logf, axis=1).transpose(0, 2, 1)
    qb = q.reshape(B, nb, Q_BLOCK, H, D).transpose(1, 0, 2, 3, 4)
    cb = ct.reshape(B, H, nb, Q_BLOCK).transpose(2, 0, 1, 3)
    kpos = jnp.arange(S)

    def block(args):
        qi, ci, i = args
        s = jnp.einsum('bqhd,bkhd->bhqk', qi, k).astype(jnp.float32) * scale
        s = s + ci[..., :, None] - ct[..., None, :]
        qpos = i * Q_BLOCK + jnp.arange(Q_BLOCK)
        s = jnp.where(kpos[None, :] <= qpos[:, None], s, -jnp.inf)
        p = jax.nn.softmax(s, axis=-1).astype(v.dtype)
        return jnp.einsum('bhqk,bkhd->bqhd', p, v)

    o = lax.map(block, (qb, cb, jnp.arange(nb)))
    return o.transpose(1, 0, 2, 3, 4).reshape(B, S, H, D)


def fox_sample(q, k_new, v_new, logf_new, k_past, v_past, logf_past):
    L = q.shape[1]
    P = k_past.shape[1]
    scale = HD_FOX ** -0.5
    lp = logf_past.astype(jnp.float32)
    rc = (lax.cumsum(lp, axis=1, reverse=True) - lp).transpose(0, 2, 1)
    cn = jnp.cumsum(logf_new, axis=1).transpose(0, 2, 1)
    s_past = jnp.einsum('bqhd,bkhd->bhqk', q, k_past).astype(jnp.float32) * scale
    s_past = s_past + cn[..., :, None] + rc[..., None, :]
    s_new = jnp.einsum('bqhd,bkhd->bhqk', q, k_new).astype(jnp.float32) * scale
    s_new = s_new + cn[..., :, None] - cn[..., None, :]
    s_new = jnp.where(jnp.tril(jnp.ones((L, L), dtype=bool)), s_new, -jnp.inf)
    p = jax.nn.softmax(jnp.concatenate([s_past, s_new], axis=-1), axis=-1).astype(v_new.dtype)
    return (jnp.einsum('bhqk,bkhd->bqhd', p[..., :P], v_past)
            + jnp.einsum('bhqk,bkhd->bqhd', p[..., P:], v_new))


def gla_chunked(q, k, v, log_a, s0):
    B, L = q.shape[:2]
    C = GLA_CHUNK if L % GLA_CHUNK == 0 else L
    nc = L // C

    def to_chunks(t):
        return t.astype(jnp.float32).reshape(B, nc, C, H_GLA, -1).transpose(1, 0, 3, 2, 4)

    qc, kc, vc, ac = to_chunks(q * DK_GLA ** -0.5), to_chunks(k), to_chunks(v), to_chunks(log_a)
    mask = jnp.tril(jnp.ones((C, C), dtype=bool))

    def step(S, inp):
        qi, ki, vi, ai = inp
        b = jnp.cumsum(ai, axis=2)
        b_last = b[:, :, -1:, :]
        qe = qi * jnp.exp(b)
        A = jnp.einsum('bhtd,bhsd->bhts', qe, ki * jnp.exp(-b))
        A = jnp.where(mask, A, 0.0)
        o = jnp.einsum('bhts,bhsv->bhtv', A, vi) + jnp.einsum('bhtd,bhdv->bhtv', qe, S)
        S = (jnp.exp(b_last)[:, :, 0, :, None] * S
             + jnp.einsum('bhsd,bhsv->bhdv', ki * jnp.exp(b_last - b), vi))
        return S, o

    S, o = lax.scan(step, s0.astype(jnp.float32), (qc, kc, vc, ac))
    return o.transpose(1, 0, 3, 2, 4).reshape(B, L, H_GLA, DV_GLA), S


def mem_attend(q, mk, mv):
    s = jnp.einsum('bqhd,bmhd->bhqm', q, mk).astype(jnp.float32) * HD_MEM ** -0.5
    p = jax.nn.softmax(s, axis=-1).astype(mv.dtype)
    return jnp.einsum('bhqm,bmhd->bqhd', p, mv)


def layer_step(x, mem_k, mem_v, gla_s0, fox_attend, g_attn, w_in, b_fox_f, w_gla_gate_up, b_gla_gate,
               g_gla_norm, w_up_fox, w_up_gla, w_up_mem, w_out, g_mlp, w_mlp_in, w_mlp_out):
    B, L, _ = x.shape
    xn = rmsnorm(x, g_attn)
    fq, fk, fv, ff, gq, gk, gv, ga, gr, mq, gl = jnp.split(xn @ w_in, SPLIT_POINTS, axis=-1)

    def heads(t, h):
        return t.reshape(B, L, h, -1)

    fk, fv = heads(fk, H_FOX), heads(fv, H_FOX)
    logf = jax.nn.log_sigmoid(ff.astype(jnp.float32) + b_fox_f.astype(jnp.float32))
    fox_o = fox_attend(heads(fq, H_FOX), fk, fv, logf).reshape(B, L, W_FOX)
    log_a = jax.nn.log_sigmoid((ga @ w_gla_gate_up + b_gla_gate).astype(jnp.float32)) / GLA_NORMALIZER
    gla_o, gla_state = gla_chunked(heads(gq, H_GLA), heads(gk, H_GLA), heads(gv, H_GLA),
                                   heads(log_a, H_GLA), gla_s0)
    gla_o = (rmsnorm(gla_o, g_gla_norm).reshape(B, L, W_GLA_V) * jax.nn.silu(gr)).astype(x.dtype)
    mem_o = mem_attend(heads(mq, H_MEM), mem_k, mem_v).reshape(B, L, W_MEM)
    gate = jax.nn.sigmoid(gl.astype(jnp.float32)).reshape(B, L, N_BRANCH, D_MODEL).astype(x.dtype)
    merged = (gate[:, :, 0] * (fox_o @ w_up_fox)
              + gate[:, :, 1] * (gla_o @ w_up_gla)
              + gate[:, :, 2] * (mem_o @ w_up_mem))
    h = x + merged @ w_out
    h = h + jnp.square(jax.nn.relu(rmsnorm(h, g_mlp) @ w_mlp_in)) @ w_mlp_out
    return h, fk, fv, logf, gla_state.astype(gla_s0.dtype)


def setup_inputs(seed: int = 0) -> dict:
    key = jax.random.key(seed)
    ks = jax.random.split(key, 32)
    n_pages = PAST_LEN // PAGE_SIZE
    n_used = DEC_BATCH * n_pages
    n_pool = n_used + n_used // 4

    def nrm(k, shape, scale=1.0):
        return jax.random.normal(k, shape, dtype=jnp.float32) * scale

    def gain(k, shape):
        return 1.0 + nrm(k, shape, 0.02)

    fox_head_bias = jnp.linspace(FOX_BIAS_LO, FOX_BIAS_HI, H_FOX, dtype=jnp.float32)
    page_table = jax.random.permutation(ks[0], n_pool)[:n_used].reshape(DEC_BATCH, n_pages).astype(jnp.int32)
    return {
        'x_prompt': nrm(ks[1], (BATCH, SEQ, D_MODEL)),
        'x_sample': nrm(ks[2], (DEC_BATCH, DEC_SEQ, D_MODEL)),
        'mem_prompt': nrm(ks[3], (BATCH, MEM_LEN, D_MODEL)),
        'cache_fox_k': nrm(ks[4], (DEPTH, n_pool, PAGE_SIZE, H_FOX, HD_FOX)),
        'cache_fox_v': nrm(ks[5], (DEPTH, n_pool, PAGE_SIZE, H_FOX, HD_FOX)),
        'cache_fox_logf': jax.nn.log_sigmoid(fox_head_bias + nrm(ks[6], (DEPTH, n_pool, PAGE_SIZE, H_FOX))),
        'state_gla': nrm(ks[7], (DEPTH, DEC_BATCH, H_GLA, DK_GLA, DV_GLA), 0.5),
        'cache_mem_k': nrm(ks[8], (DEPTH, DEC_BATCH, MEM_LEN, H_MEM, HD_MEM)),
        'cache_mem_v': nrm(ks[9], (DEPTH, DEC_BATCH, MEM_LEN, H_MEM, HD_MEM)),
        'page_table': page_table,
        'g_attn': gain(ks[10], (DEPTH, D_MODEL)),
        'w_in': nrm(ks[11], (DEPTH, D_MODEL, D_IN), D_MODEL ** -0.5),
        'b_fox_f': fox_head_bias + nrm(ks[12], (DEPTH, H_FOX), 0.5),
        'w_gla_gate_up': nrm(ks[13], (DEPTH, GLA_RANK, W_GLA_K), GLA_RANK ** -0.5),
        'b_gla_gate': 1.0 + nrm(ks[14], (DEPTH, W_GLA_K), 0.1),
        'g_gla_norm': gain(ks[15], (DEPTH, DV_GLA)),
        'g_mem': gain(ks[16], (DEPTH, D_MODEL)),
        'w_mem_k': nrm(ks[17], (DEPTH, D_MODEL, W_MEM), D_MODEL ** -0.5),
        'w_mem_v': nrm(ks[18], (DEPTH, D_MODEL, W_MEM), D_MODEL ** -0.5),
        'w_up_fox': nrm(ks[19], (DEPTH, W_FOX, D_MODEL), W_FOX ** -0.5),
        'w_up_gla': nrm(ks[20], (DEPTH, W_GLA_V, D_MODEL), W_GLA_V ** -0.5),
        'w_up_mem': nrm(ks[21], (DEPTH, W_MEM, D_MODEL), W_MEM ** -0.5),
        'w_out': nrm(ks[22], (DEPTH, D_MODEL, D_MODEL), D_MODEL ** -0.5),
        'g_mlp': gain(ks[23], (DEPTH, D_MODEL)),
        'w_mlp_in': nrm(ks[24], (DEPTH, D_MODEL, D_FF), D_MODEL ** -0.5),
        'w_mlp_out': nrm(ks[25], (DEPTH, D_FF, D_MODEL), D_FF ** -0.5),
        'g_final': gain(ks[26], (D_MODEL,)),
    }


def reference(x_prompt, x_sample, mem_prompt, cache_fox_k, cache_fox_v, cache_fox_logf, state_gla,
              cache_mem_k, cache_mem_v, page_table, g_attn, w_in, b_fox_f, w_gla_gate_up, b_gla_gate,
              g_gla_norm, g_mem, w_mem_k, w_mem_v, w_up_fox, w_up_gla, w_up_mem, w_out, g_mlp,
              w_mlp_in, w_mlp_out, g_final):
    B = x_prompt.shape[0]
    DB = x_sample.shape[0]
    xp, xs = x_prompt, x_sample
    pk, pv, plf, pst, pmk, pmv = [], [], [], [], [], []
    sk, sv, slf, sst = [], [], [], []
    for l in range(DEPTH):
        prm = (g_attn[l], w_in[l], b_fox_f[l], w_gla_gate_up[l], b_gla_gate[l], g_gla_norm[l],
               w_up_fox[l], w_up_gla[l], w_up_mem[l], w_out[l], g_mlp[l], w_mlp_in[l], w_mlp_out[l])
        mn = rmsnorm(mem_prompt, g_mem[l])
        mk = (mn @ w_mem_k[l]).reshape(B, MEM_LEN, H_MEM, HD_MEM)
        mv = (mn @ w_mem_v[l]).reshape(B, MEM_LEN, H_MEM, HD_MEM)
        s0 = jnp.zeros((B, H_GLA, DK_GLA, DV_GLA), dtype=state_gla.dtype)
        xp, k_, v_, lf_, st_ = layer_step(xp, mk, mv, s0, fox_prompt, *prm)
        pk.append(k_); pv.append(v_); plf.append(lf_); pst.append(st_); pmk.append(mk); pmv.append(mv)
        k_past = cache_fox_k[l][page_table].reshape(DB, -1, H_FOX, HD_FOX)
        v_past = cache_fox_v[l][page_table].reshape(DB, -1, H_FOX, HD_FOX)
        lf_past = cache_fox_logf[l][page_table].reshape(DB, -1, H_FOX)
        attend = functools.partial(fox_sample, k_past=k_past, v_past=v_past, logf_past=lf_past)
        xs, k_, v_, lf_, st_ = layer_step(xs, cache_mem_k[l], cache_mem_v[l], state_gla[l], attend, *prm)
        sk.append(k_); sv.append(v_); slf.append(lf_); sst.append(st_)
    y_prompt = rmsnorm(xp, g_final)
    y_sample = rmsnorm(xs, g_final)
    p_fox_k, p_fox_v, p_fox_logf = jnp.stack(pk), jnp.stack(pv), jnp.stack(plf)
    p_gla_state, p_mem_k, p_mem_v = jnp.stack(pst), jnp.stack(pmk), jnp.stack(pmv)
    s_fox_k, s_fox_v, s_fox_logf, s_gla_state = jnp.stack(sk), jnp.stack(sv), jnp.stack(slf), jnp.stack(sst)
    return (y_prompt, y_sample, p_fox_k, p_fox_v, p_fox_logf, p_gla_state, p_mem_k, p_mem_v,
            s_fox_k, s_fox_v, s_fox_logf, s_gla_state)
```

```python
import functools

import jax
import jax.numpy as jnp
from jax import lax
from jax.experimental import pallas as pl
from jax.experimental.pallas import tpu as pltpu

F32 = jnp.float32
BF16 = jnp.bfloat16

H_FOX = 16
HD_FOX = 64
W_FOX = H_FOX * HD_FOX
H_GLA = 4
DK_GLA = 128
DV_GLA = 256
W_GLA_K = H_GLA * DK_GLA
W_GLA_V = H_GLA * DV_GLA
GLA_RANK = 16
GLA_NORMALIZER = 16.0
GLA_CHUNK = 64
MEM_LEN = 256
H_MEM = 4
HD_MEM = 64
W_MEM = H_MEM * HD_MEM
N_BRANCH = 3
EPS = 1e-6

LANES = 128
FOX_AUG = 128
VMEM_LIMIT = 56 * 1024 * 1024


def _dot(a, b):
    return jnp.dot(a, b, preferred_element_type=F32)


def _dot_nt(a, b):
    return lax.dot_general(a, b, (((1,), (1,)), ((), ())), preferred_element_type=F32)


def _split_bf16(x, parts):
    out = []
    r = x
    for _ in range(parts):
        p = r.astype(BF16)
        out.append(p)
        r = r - p.astype(F32)
    return out


def _log_sigmoid(x):
    return jnp.minimum(x, 0.0) - jnp.log1p(jnp.exp(-jnp.abs(x)))


def _sigmoid(x):
    return 1.0 / (1.0 + jnp.exp(-x))


def _rmsnorm(x, g):
    ms = jnp.mean(x * x, axis=-1, keepdims=True)
    return x * lax.rsqrt(ms + EPS) * g


def _lane_cumsum(x, reverse=False):
    rows, n = x.shape
    lane = lax.broadcasted_iota(jnp.int32, (rows, LANES), 1)
    blocks = []
    for i in range(n // LANES):
        y = x[:, i * LANES:(i + 1) * LANES]
        k = 1
        while k < LANES:
            if reverse:
                y = y + jnp.where(lane < LANES - k, pltpu.roll(y, LANES - k, 1), 0.0)
            else:
                y = y + jnp.where(lane >= k, pltpu.roll(y, k, 1), 0.0)
            k *= 2
        blocks.append(y)
    order = range(len(blocks) - 1, -1, -1) if reverse else range(len(blocks))
    edge = 0 if reverse else LANES - 1
    carry = None
    for i in order:
        if carry is not None:
            blocks[i] = blocks[i] + carry
        carry = blocks[i][:, edge:edge + 1]
    return blocks[0] if len(blocks) == 1 else jnp.concatenate(blocks, axis=1)


def _const_spec(shape):
    return pl.BlockSpec(shape, lambda *_: (0,) * len(shape))


def _params(sem):
    return pltpu.CompilerParams(dimension_semantics=sem, vmem_limit_bytes=VMEM_LIMIT)


def _fox_proj_kernel(x_ref, g_ref, wq_ref, wkT_ref, wvT_ref, wff_ref, wffT_ref, brow_ref, bcol_ref,
                     eq_ref, ekT_ref,
                     xn_ref, qa_ref, kaT_ref, kT_ref, vT_ref, vT16_ref, lfT_ref,
                     crow_ref, ccol_ref):
    i = pl.program_id(1)
    ts = x_ref.shape[1]

    @pl.when(i == 0)
    def _():
        crow_ref[...] = jnp.zeros_like(crow_ref)
        ccol_ref[...] = jnp.zeros_like(ccol_ref)

    xn = _rmsnorm(x_ref[0], g_ref[...]).astype(BF16)
    xn_ref[0] = xn
    kT = _dot_nt(wkT_ref[...], xn)
    vT = _dot_nt(wvT_ref[...], xn)
    kT_ref[0] = kT
    vT_ref[0] = vT
    vT16_ref[0] = vT.astype(BF16)

    lf = _log_sigmoid(_dot(xn, wff_ref[...]) + brow_ref[...])
    lfT = _log_sigmoid(_dot_nt(wffT_ref[...], xn) + bcol_ref[...])
    lfT_ref[0] = lfT

    cT = _lane_cumsum(lfT) + ccol_ref[...]
    ccol_ref[...] = cT[:, ts - 1:ts]
    r = lax.broadcasted_iota(jnp.int32, (ts, ts), 0)
    c_ = lax.broadcasted_iota(jnp.int32, (ts, ts), 1)
    tril = (c_ <= r).astype(BF16)
    c = crow_ref[...]
    for part in _split_bf16(lf, 3):
        c = c + _dot(tril, part)
    crow_ref[...] = c[ts - 1:ts, :]

    q = _dot(xn, wq_ref[...]) * (HD_FOX ** -0.5)
    ex = None
    for k, part in enumerate(_split_bf16(c, 3)):
        t = _dot(part, eq_ref[k])
        ex = t if ex is None else ex + t
    ex = ex + eq_ref[3][0:1, :].astype(F32)
    lane = lax.broadcasted_iota(jnp.int32, (ts, LANES), 1)
    lo = lane < HD_FOX
    for p in range(H_FOX // 2):
        qg = q[:, p * LANES:(p + 1) * LANES]
        eg = ex[:, p * LANES:(p + 1) * LANES]
        qa_ref[0, :, (2 * p) * FOX_AUG:(2 * p + 1) * FOX_AUG] = jnp.where(
            lo, qg, pltpu.roll(eg, HD_FOX, 1)).astype(BF16)
        qa_ref[0, :, (2 * p + 1) * FOX_AUG:(2 * p + 2) * FOX_AUG] = jnp.where(
            lo, pltpu.roll(qg, HD_FOX, 1), eg).astype(BF16)

    exT = None
    for k, part in enumerate(_split_bf16(-cT, 3)):
        t = _dot(ekT_ref[k], part)
        exT = t if exT is None else exT + t
    exT = exT + ekT_ref[3][:, 0:1].astype(F32)
    for h in range(H_FOX):
        kaT_ref[0, h * FOX_AUG:h * FOX_AUG + HD_FOX, :] = kT[h * HD_FOX:(h + 1) * HD_FOX, :].astype(BF16)
        kaT_ref[0, h * FOX_AUG + HD_FOX:(h + 1) * FOX_AUG, :] = exT[h * HD_FOX:(h + 1) * HD_FOX, :].astype(BF16)


def _fox_aug_selectors():
    h = jnp.arange(H_FOX)
    eq = jnp.zeros((4, H_FOX, W_FOX), F32)
    ek = jnp.zeros((4, W_FOX, H_FOX), F32)
    for k in range(3):
        eq = eq.at[k, h, h * HD_FOX + k].set(1.0)
        eq = eq.at[3, :, h * HD_FOX + 3 + k].set(1.0)
        ek = ek.at[k, h * HD_FOX + 3 + k, h].set(1.0)
        ek = ek.at[3, h * HD_FOX + k, :].set(1.0)
    return eq.astype(BF16), ek.astype(BF16)


def fox_proj(x, g, wq, wkT, wvT, wff, wffT, brow, bcol, ts):
    B, S, D = x.shape
    eq, ekT = _fox_aug_selectors()
    out_shape = (
        jax.ShapeDtypeStruct((B, S, D), BF16),
        jax.ShapeDtypeStruct((B, S, H_FOX * FOX_AUG), BF16),
        jax.ShapeDtypeStruct((B, H_FOX * FOX_AUG, S), BF16),
        jax.ShapeDtypeStruct((B, W_FOX, S), F32),
        jax.ShapeDtypeStruct((B, W_FOX, S), F32),
        jax.ShapeDtypeStruct((B, W_FOX, S), BF16),
        jax.ShapeDtypeStruct((B, H_FOX, S), F32),
    )
    tile = lambda n: pl.BlockSpec((1, ts, n), lambda b, i: (b, i, 0))
    tileT = lambda n: pl.BlockSpec((1, n, ts), lambda b, i: (b, 0, i))
    return pl.pallas_call(
        _fox_proj_kernel,
        grid=(B, S // ts),
        in_specs=[tile(D), _const_spec((1, D)), _const_spec(wq.shape), _const_spec(wkT.shape),
                  _const_spec(wvT.shape), _const_spec(wff.shape), _const_spec(wffT.shape),
                  _const_spec(brow.shape), _const_spec(bcol.shape), _const_spec(eq.shape),
                  _const_spec(ekT.shape)],
        out_specs=(tile(D), tile(H_FOX * FOX_AUG), tileT(H_FOX * FOX_AUG), tileT(W_FOX), tileT(W_FOX),
                   tileT(W_FOX), tileT(H_FOX)),
        out_shape=out_shape,
        scratch_shapes=[pltpu.VMEM((1, H_FOX), F32), pltpu.VMEM((H_FOX, 1), F32)],
        compiler_params=_params(("parallel", "arbitrary")),
        name="fox_proj",
    )(x, g, wq, wkT, wvT, wff, wffT, brow, bcol, eq, ekT)


def _fox_attn_kernel(qa_ref, kaT_ref, vT_ref, o_ref, m_ref, l_ref, acc_ref):
    i = pl.program_id(2)
    tq = qa_ref.shape[1]
    tk = tq
    rep = tk // LANES

    row = lax.broadcasted_iota(jnp.int32, (tq, tk), 0)
    col = lax.broadcasted_iota(jnp.int32, (tq, tk), 1)
    causal = col <= row

    for h in range(2):
        m_ref[h] = jnp.full((tq, LANES), -jnp.inf, F32)
        l_ref[h] = jnp.zeros((tq, LANES), F32)
        acc_ref[h] = jnp.zeros((tq, HD_FOX), F32)

    def step(j, masked):
        off = pl.multiple_of(j * tk, tk)
        for h in range(2):
            qa = qa_ref[0, :, h * FOX_AUG:(h + 1) * FOX_AUG]
            ka = kaT_ref[0, h * FOX_AUG:(h + 1) * FOX_AUG, pl.ds(off, tk)]
            s = _dot(qa, ka)
            if masked:
                s = jnp.where(causal, s, -jnp.inf)
            m_prev = m_ref[h]
            m_next = jnp.maximum(m_prev, jnp.max(s, axis=1, keepdims=True))
            p = jnp.exp(s - jnp.concatenate([m_next] * rep, axis=1))
            alpha = jnp.exp(m_prev - m_next)
            l_ref[h] = alpha * l_ref[h] + jnp.sum(p, axis=1, keepdims=True)
            m_ref[h] = m_next
            v = vT_ref[0, h * HD_FOX:(h + 1) * HD_FOX, pl.ds(off, tk)]
            acc_ref[h] = alpha[:, :HD_FOX] * acc_ref[h] + _dot_nt(p.astype(BF16), v)

    def body(j, carry):
        step(j, False)
        return carry

    lax.fori_loop(0, i, body, 0)
    step(i, True)

    outs = [acc_ref[h] / l_ref[h][:, :HD_FOX] for h in range(2)]
    o_ref[0] = jnp.concatenate(outs, axis=1).astype(o_ref.dtype)


def fox_attn(qa, kaT, vT16, tq):
    B, S, _ = qa.shape
    return pl.pallas_call(
        _fox_attn_kernel,
        grid=(B, H_FOX // 2, S // tq),
        in_specs=[pl.BlockSpec((1, tq, 2 * FOX_AUG), lambda b, p, i: (b, i, p)),
                  pl.BlockSpec((1, 2 * FOX_AUG, S), lambda b, p, i: (b, p, 0)),
                  pl.BlockSpec((1, 2 * HD_FOX, S), lambda b, p, i: (b, p, 0))],
        out_specs=pl.BlockSpec((1, tq, 2 * HD_FOX), lambda b, p, i: (b, i, p)),
        out_shape=jax.ShapeDtypeStruct((B, S, W_FOX), BF16),
        scratch_shapes=[pltpu.VMEM((2, tq, LANES), F32), pltpu.VMEM((2, tq, LANES), F32),
                        pltpu.VMEM((2, tq, HD_FOX), F32)],
        compiler_params=_params(("parallel", "parallel", "arbitrary")),
        name="fox_attn",
    )(qa, kaT, vT16)


def _gla_kernel(xn_ref, wq_ref, wkT_ref, wv_ref, wa_ref, waT_ref, wr_ref, wu_ref, wuT_ref,
                brow_ref, bcol_ref, g_ref, o_ref, st_ref, s_ref, oacc_ref):
    t = pl.program_id(1)
    T = xn_ref.shape[1]
    C = GLA_CHUNK

    @pl.when(t == 0)
    def _():
        s_ref[...] = jnp.zeros_like(s_ref)

    xn = xn_ref[0]
    q = _dot(xn, wq_ref[...]) * (DK_GLA ** -0.5)
    kT = _dot_nt(wkT_ref[...], xn)
    v = _dot(xn, wv_ref[...]).astype(BF16)
    ga = _dot(xn, wa_ref[...]).astype(BF16)
    gaT = _dot_nt(waT_ref[...], xn).astype(BF16)
    la = _log_sigmoid(_dot(ga, wu_ref[...]) + brow_ref[...]) / GLA_NORMALIZER
    laT = _log_sigmoid(_dot(wuT_ref[...], gaT) + bcol_ref[...]) / GLA_NORMALIZER

    r = lax.broadcasted_iota(jnp.int32, (T, T), 0)
    c_ = lax.broadcasted_iota(jnp.int32, (T, T), 1)
    same = (r // C) == (c_ // C)
    lower = jnp.logical_and(same, c_ <= r).astype(BF16)
    upper = jnp.logical_and(same, r <= c_).astype(BF16)
    block = same.astype(BF16)
    b = bT = totT = None
    for part in _split_bf16(la, 2):
        d = _dot(lower, part)
        b = d if b is None else b + d
    for part in _split_bf16(laT, 2):
        d = _dot(part, upper)
        bT = d if bT is None else bT + d
        d = _dot(part, block)
        totT = d if totT is None else totT + d

    qe = (q * jnp.exp(b)).astype(BF16)
    kdT = (kT * jnp.exp(-bT)).astype(BF16)
    keT = (kT * jnp.exp(totT - bT)).astype(BF16)
    decT = jnp.exp(totT)

    ri = lax.broadcasted_iota(jnp.int32, (C, C), 0)
    ci = lax.broadcasted_iota(jnp.int32, (C, C), 1)
    tril = ci <= ri
    for c in range(T // C):
        rows = slice(c * C, (c + 1) * C)
        for h in range(H_GLA):
            kk = slice(h * DK_GLA, (h + 1) * DK_GLA)
            vv = slice(h * DV_GLA, (h + 1) * DV_GLA)
            qe_c = qe[rows, kk]
            v_c = v[rows, vv]
            a = jnp.where(tril, _dot(qe_c, kdT[kk, rows]), 0.0)
            s_old = s_ref[h]
            oacc_ref[rows, vv] = _dot(a.astype(BF16), v_c) + _dot(qe_c, s_old.astype(BF16))
            dec = decT[kk, c * C:c * C + 1]
            s_ref[h] = dec * s_old + _dot(keT[kk, rows], v_c)

    gr = _dot(xn, wr_ref[...])
    gate = gr * _sigmoid(gr)
    for h in range(H_GLA):
        vv = slice(h * DV_GLA, (h + 1) * DV_GLA)
        y = _rmsnorm(oacc_ref[:, vv], g_ref[...])
        o_ref[0, :, vv] = (y * gate[:, vv]).astype(o_ref.dtype)

    @pl.when(t == pl.num_programs(1) - 1)
    def _():
        st_ref[0] = s_ref[...]


def gla(xn, wq, wkT, wv, wa, waT, wr, wu, wuT, brow, bcol, g, T):
    B, S, D = xn.shape
    consts = (wq, wkT, wv, wa, waT, wr, wu, wuT, brow, bcol, g)
    return pl.pallas_call(
        _gla_kernel,
        grid=(B, S // T),
        in_specs=[pl.BlockSpec((1, T, D), lambda b, t: (b, t, 0))] + [_const_spec(w.shape) for w in consts],
        out_specs=(pl.BlockSpec((1, T, W_GLA_V), lambda b, t: (b, t, 0)),
                   pl.BlockSpec((1, H_GLA, DK_GLA, DV_GLA), lambda b, t: (b, 0, 0, 0))),
        out_shape=(jax.ShapeDtypeStruct((B, S, W_GLA_V), BF16),
                   jax.ShapeDtypeStruct((B, H_GLA, DK_GLA, DV_GLA), F32)),
        scratch_shapes=[pltpu.VMEM((H_GLA, DK_GLA, DV_GLA), F32), pltpu.VMEM((T, W_GLA_V), F32)],
        compiler_params=_params(("parallel", "arbitrary")),
        name="gla",
    )(xn, *consts)


def _mem_kv_kernel(m_ref, g_ref, wkT_ref, wvT_ref, kT_ref, vT_ref):
    mn = _rmsnorm(m_ref[0], g_ref[...]).astype(BF16)
    kT_ref[0] = _dot_nt(wkT_ref[...], mn)
    vT_ref[0] = _dot_nt(wvT_ref[...], mn)


def mem_kv(mem, g, wkT, wvT):
    B, M, D = mem.shape
    blk = pl.BlockSpec((1, W_MEM, M), lambda b: (b, 0, 0))
    return pl.pallas_call(
        _mem_kv_kernel,
        grid=(B,),
        in_specs=[pl.BlockSpec((1, M, D), lambda b: (b, 0, 0)), _const_spec(g.shape),
                  _const_spec(wkT.shape), _const_spec(wvT.shape)],
        out_specs=(blk, blk),
        out_shape=(jax.ShapeDtypeStruct((B, W_MEM, M), F32),) * 2,
        compiler_params=_params(("parallel",)),
        name="mem_kv",
    )(mem, g, wkT, wvT)


def _mem_attn_kernel(xn_ref, wq_ref, kT_ref, vT_ref, o_ref):
    q = _dot(xn_ref[0], wq_ref[...]).astype(BF16)
    outs = []
    for h in range(H_MEM):
        hh = slice(h * HD_MEM, (h + 1) * HD_MEM)
        s = _dot(q[:, hh], kT_ref[0, hh, :].astype(BF16)) * (HD_MEM ** -0.5)
        p = jnp.exp(s - jnp.max(s, axis=1, keepdims=True))
        p = p / jnp.sum(p, axis=1, keepdims=True)
        outs.append(_dot_nt(p.astype(BF16), vT_ref[0, hh, :].astype(BF16)))
    o_ref[0] = jnp.concatenate(outs, axis=1).astype(o_ref.dtype)


def mem_attn(xn, wq, kT, vT, tl):
    B, L, D = xn.shape
    M = kT.shape[2]
    return pl.pallas_call(
        _mem_attn_kernel,
        grid=(B, L // tl),
        in_specs=[pl.BlockSpec((1, tl, D), lambda b, i: (b, i, 0)), _const_spec(wq.shape),
                  pl.BlockSpec((1, W_MEM, M), lambda b, i: (b, 0, 0)),
                  pl.BlockSpec((1, W_MEM, M), lambda b, i: (b, 0, 0))],
        out_specs=pl.BlockSpec((1, tl, W_MEM), lambda b, i: (b, i, 0)),
        out_shape=jax.ShapeDtypeStruct((B, L, W_MEM), BF16),
        compiler_params=_params(("parallel", "parallel")),
        name="mem_attn",
    )(xn, wq, kT, vT)


def _merge_kernel(x_ref, xn_ref, fo_ref, go_ref, mo_ref, wgl_ref, wuf_ref, wug_ref, wum_ref, wout_ref, h_ref):
    D = x_ref.shape[1]
    xn = xn_ref[...]
    merged = None
    for k, (br, w) in enumerate(((fo_ref, wuf_ref), (go_ref, wug_ref), (mo_ref, wum_ref))):
        gate = _sigmoid(_dot(xn, wgl_ref[:, k * D:(k + 1) * D]))
        t = gate * _dot(br[...], w[...])
        merged = t if merged is None else merged + t
    h_ref[...] = x_ref[...] + _dot(merged.astype(BF16), wout_ref[...])


def merge(x, xn, fo, go, mo, wgl, wuf, wug, wum, wout, tm):
    N, D = x.shape
    row = lambda n: pl.BlockSpec((tm, n), lambda i: (i, 0))
    consts = (wgl, wuf, wug, wum, wout)
    return pl.pallas_call(
        _merge_kernel,
        grid=(N // tm,),
        in_specs=[row(D), row(D), row(fo.shape[1]), row(go.shape[1]), row(mo.shape[1])]
                 + [_const_spec(w.shape) for w in consts],
        out_specs=row(D),
        out_shape=jax.ShapeDtypeStruct((N, D), F32),
        compiler_params=_params(("parallel",)),
        name="merge",
    )(x, xn, fo, go, mo, *consts)


FF_CHUNK = 1024


def _mlp_kernel(h_ref, g_ref, w1_ref, w2_ref, gf_ref, y_ref):
    h = h_ref[...]
    hn = _rmsnorm(h, g_ref[...]).astype(BF16)
    acc = h
    for c in range(w1_ref.shape[1] // FF_CHUNK):
        cc = slice(c * FF_CHUNK, (c + 1) * FF_CHUNK)
        u = jnp.maximum(_dot(hn, w1_ref[:, cc]), 0.0)
        acc = acc + _dot((u * u).astype(BF16), w2_ref[cc, :])
    y_ref[...] = _rmsnorm(acc, gf_ref[...])


def mlp(h, g, w1, w2, gf, tm):
    N, D = h.shape
    row = pl.BlockSpec((tm, D), lambda i: (i, 0))
    return pl.pallas_call(
        _mlp_kernel,
        grid=(N // tm,),
        in_specs=[row, _const_spec(g.shape), _const_spec(w1.shape), _const_spec(w2.shape),
                  _const_spec(gf.shape)],
        out_specs=row,
        out_shape=jax.ShapeDtypeStruct((N, D), F32),
        compiler_params=_params(("parallel",)),
        name="mlp",
    )(h, g, w1, w2, gf)


def _sample_proj_kernel(x_ref, g_ref, w_ref, xn_ref, o_ref):
    xn = _rmsnorm(x_ref[...], g_ref[...]).astype(BF16)
    xn_ref[...] = xn
    o_ref[...] = _dot(xn, w_ref[...])


def sample_proj(x, g, w, tn):
    N, D = x.shape
    NP = w.shape[1]
    return pl.pallas_call(
        _sample_proj_kernel,
        grid=(NP // tn,),
        in_specs=[_const_spec((N, D)), _const_spec(g.shape), pl.BlockSpec((D, tn), lambda j: (0, j))],
        out_specs=(_const_spec((N, D)), pl.BlockSpec((N, tn), lambda j: (0, j))),
        out_shape=(jax.ShapeDtypeStruct((N, D), BF16), jax.ShapeDtypeStruct((N, NP), F32)),
        compiler_params=_params(("arbitrary",)),
        name="sample_proj",
    )(x, g, w)


def _logsig_kernel(ff_ref, b_ref, o_ref):
    o_ref[...] = _log_sigmoid(ff_ref[...] + b_ref[...])


def logsig_bias(ff, b):
    return pl.pallas_call(_logsig_kernel, out_shape=jax.ShapeDtypeStruct(ff.shape, F32), name="logsig")(ff, b)


def _decode_kernel(pt_ref, q_ref, kn_ref, vn_ref, lfn_ref, *refs, pages):
    k_refs, v_refs, lf_refs = refs[:pages], refs[pages:2 * pages], refs[2 * pages:3 * pages]
    o_ref, m_ref, l_ref, r_ref, acc_ref = refs[3 * pages:]
    j = pl.program_id(1)
    q = q_ref[0] * (HD_FOX ** -0.5)
    lane = lax.broadcasted_iota(jnp.int32, (H_FOX, LANES), 1)
    lane_w = lax.broadcasted_iota(jnp.int32, (W_FOX, LANES), 1)

    def head_scores(k):
        prod = k * q
        return jnp.concatenate(
            [jnp.sum(prod[h * HD_FOX:(h + 1) * HD_FOX], axis=0, keepdims=True) for h in range(H_FOX)], axis=0)

    @pl.when(j == 0)
    def _():
        m_ref[...] = head_scores(kn_ref[0])
        l_ref[...] = jnp.where(lane == 0, 1.0, 0.0)
        r_ref[...] = jnp.zeros_like(r_ref)
        acc_ref[...] = jnp.where(lane_w == 0, vn_ref[0], 0.0)

    for g in range(pages):
        lf = lf_refs[g][0]
        incl = _lane_cumsum(lf, reverse=True)
        r_prev = r_ref[...]
        s = head_scores(k_refs[g][0]) + lfn_ref[0] + (incl - lf + r_prev)
        r_ref[...] = r_prev + incl[:, 0:1]
        m_prev = m_ref[...]
        m_new = jnp.maximum(m_prev, jnp.max(s, axis=1, keepdims=True))
        alpha = jnp.exp(m_prev - m_new)
        p = jnp.exp(s - m_new)
        l_ref[...] = alpha * l_ref[...] + p
        m_ref[...] = m_new
        for h in range(H_FOX):
            rows = slice(h * HD_FOX, (h + 1) * HD_FOX)
            acc_ref[rows, :] = acc_ref[rows, :] * alpha[h:h + 1, :] + v_refs[g][0, rows, :] * p[h:h + 1, :]

    @pl.when(j == pl.num_programs(1) - 1)
    def _():
        inv = 1.0 / jnp.sum(l_ref[...], axis=1, keepdims=True)
        for h in range(H_FOX):
            rows = slice(h * HD_FOX, (h + 1) * HD_FOX)
            acc_ref[rows, :] = acc_ref[rows, :] * inv[h:h + 1, :]
        ones = jnp.ones((8, LANES), BF16)
        out = None
        for part in _split_bf16(acc_ref[...], 3):
            t = _dot_nt(ones, part)
            out = t if out is None else out + t
        o_ref[0] = out[0:1, :]


def decode_attn(page_table, q_rep, kn_rep, vn_rep, lfn_rep, kT_pool, vT_pool, lfT_pool, pages):
    DB = q_rep.shape[0]
    P = page_table.shape[1]
    per_b = lambda n: pl.BlockSpec((1, n, LANES), lambda b, j, pt: (b, 0, 0))

    def paged(n, g):
        return pl.BlockSpec((1, n, LANES), lambda b, j, pt: (pt[b, P - 1 - (j * pages + g)], 0, 0))

    grid_spec = pltpu.PrefetchScalarGridSpec(
        num_scalar_prefetch=1,
        grid=(DB, P // pages),
        in_specs=[per_b(W_FOX), per_b(W_FOX), per_b(W_FOX), per_b(H_FOX)]
                 + [paged(W_FOX, g) for g in range(pages)]
                 + [paged(W_FOX, g) for g in range(pages)]
                 + [paged(H_FOX, g) for g in range(pages)],
        out_specs=pl.BlockSpec((1, 1, W_FOX), lambda b, j, pt: (b, 0, 0)),
        scratch_shapes=[pltpu.VMEM((H_FOX, LANES), F32), pltpu.VMEM((H_FOX, LANES), F32),
                        pltpu.VMEM((H_FOX, LANES), F32), pltpu.VMEM((W_FOX, LANES), F32)],
    )
    return pl.pallas_call(
        functools.partial(_decode_kernel, pages=pages),
        grid_spec=grid_spec,
        out_shape=jax.ShapeDtypeStruct((DB, 1, W_FOX), F32),
        compiler_params=_params(("parallel", "arbitrary")),
        name="decode_attn",
    )(page_table, q_rep, kn_rep, vn_rep, lfn_rep, *([kT_pool] * pages), *([vT_pool] * pages),
      *([lfT_pool] * pages))


def _sample_gla_kernel(q_ref, k_ref, ga_ref, v_ref, gr_ref, s0_ref, wuT_ref, bcol_ref, g_ref, o_ref, st_ref):
    z = _dot(wuT_ref[...], ga_ref[0].astype(BF16)) + bcol_ref[...]
    la = _log_sigmoid(z) / GLA_NORMALIZER
    ea = jnp.exp(la)
    k = k_ref[0]
    qe = q_ref[0] * (DK_GLA ** -0.5) * ea
    kd = k * jnp.exp(-la)
    gr = gr_ref[0]
    gate = gr * _sigmoid(gr)
    wide = lambda a: jnp.concatenate([a, a], axis=1)
    for h in range(H_GLA):
        kk = slice(h * DK_GLA, (h + 1) * DK_GLA)
        vv = slice(h * DV_GLA, (h + 1) * DV_GLA)
        s_old = s0_ref[0, h]
        v_h = v_ref[0][:, vv]
        a = jnp.sum(qe[kk] * kd[kk], axis=0, keepdims=True)
        o = wide(a) * v_h + jnp.sum(wide(qe[kk]) * s_old, axis=0, keepdims=True)
        st_ref[0, h] = wide(ea[kk]) * s_old + wide(k[kk]) * v_h
        o_ref[0, :, vv] = _rmsnorm(o, g_ref[...]) * gate[:, vv]


def sample_gla(q_rep, k_rep, ga_rep, v, gr, s0, wuT, bcol, g):
    DB = q_rep.shape[0]
    per_b = lambda a: pl.BlockSpec((1,) + a.shape[1:], lambda b: (b,) + (0,) * (a.ndim - 1))
    return pl.pallas_call(
        _sample_gla_kernel,
        grid=(DB,),
        in_specs=[per_b(q_rep), per_b(k_rep), per_b(ga_rep), per_b(v), per_b(gr), per_b(s0),
                  _const_spec(wuT.shape), _const_spec(bcol.shape), _const_spec(g.shape)],
        out_specs=(per_b(v), per_b(s0)),
        out_shape=(jax.ShapeDtypeStruct(v.shape, F32), jax.ShapeDtypeStruct(s0.shape, F32)),
        compiler_params=_params(("parallel",)),
        name="sample_gla",
    )(q_rep, k_rep, ga_rep, v, gr, s0, wuT, bcol, g)


_SPLITS = (W_FOX, W_FOX, W_FOX, H_FOX, W_GLA_K, W_GLA_K, W_GLA_V, GLA_RANK, W_GLA_V, W_MEM)
PROMPT_TILE = 512
GLA_TILE = 256
DECODE_PAGES = 4
SAMPLE_PROJ_TILE = 512
SAMPLE_MEM_ROWS = 8


def _lane_rep(a):
    return jnp.broadcast_to(a[..., None], a.shape + (LANES,))


def kernel(x_prompt, x_sample, mem_prompt, cache_fox_k, cache_fox_v, cache_fox_logf, state_gla, cache_mem_k, cache_mem_v, page_table, g_attn, w_in, b_fox_f, w_gla_gate_up, b_gla_gate, g_gla_norm, g_mem, w_mem_k, w_mem_v, w_up_fox, w_up_gla, w_up_mem, w_out, g_mlp, w_mlp_in, w_mlp_out, g_final):
    B, S, D = x_prompt.shape
    DB = x_sample.shape[0]
    assert w_in.shape[0] == 1 and x_sample.shape[1] == 1

    w = w_in[0]
    segs, off = [], 0
    for n in _SPLITS:
        segs.append(w[:, off:off + n])
        off += n
    w_fq, w_fk, w_fv, w_ff, w_gq, w_gk, w_gv, w_ga, w_gr, w_mq = segs
    w_gl = w[:, off:]
    bf = lambda a: a.astype(BF16)
    bfT = lambda a: a.T.astype(BF16)
    row = lambda a: a.reshape(1, -1)
    col = lambda a: a.reshape(-1, 1)

    ga_args = (bf(w_gq), bfT(w_gk), bf(w_gv), bf(w_ga), bfT(w_ga), bf(w_gr), bf(w_gla_gate_up[0]),
               bfT(w_gla_gate_up[0]), row(b_gla_gate[0]), col(b_gla_gate[0]), row(g_gla_norm[0]))
    merge_w = (bf(w_gl), bf(w_up_fox[0]), bf(w_up_gla[0]), bf(w_up_mem[0]), bf(w_out[0]))
    mlp_w = (row(g_mlp[0]), bf(w_mlp_in[0]), bf(w_mlp_out[0]), row(g_final))
    g_attn_row = row(g_attn[0])

    xn, qa, kaT, kT, vT, vT16, lfT = fox_proj(
        x_prompt, g_attn_row, bf(w_fq), bfT(w_fk), bfT(w_fv), bf(w_ff), bfT(w_ff),
        row(b_fox_f[0]), col(b_fox_f[0]), PROMPT_TILE)
    fox_o = fox_attn(qa, kaT, vT16, PROMPT_TILE)
    gla_o, p_state = gla(xn, *ga_args, GLA_TILE)
    mkT, mvT = mem_kv(mem_prompt, row(g_mem[0]), bfT(w_mem_k[0]), bfT(w_mem_v[0]))
    mem_o = mem_attn(xn, bf(w_mq), mkT, mvT, PROMPT_TILE)
    N = B * S
    h = merge(x_prompt.reshape(N, D), xn.reshape(N, D), fox_o.reshape(N, W_FOX), gla_o.reshape(N, W_GLA_V),
              mem_o.reshape(N, W_MEM), *merge_w, PROMPT_TILE)
    y_prompt = mlp(h, *mlp_w, PROMPT_TILE).reshape(B, S, D)

    heads_out = lambda aT, nh, hd: aT.reshape(aT.shape[0], nh, hd, aT.shape[2]).transpose(0, 3, 1, 2)[None]
    p_fox_k = heads_out(kT, H_FOX, HD_FOX)
    p_fox_v = heads_out(vT, H_FOX, HD_FOX)
    p_fox_logf = lfT.transpose(0, 2, 1)[None]
    p_mem_k = heads_out(mkT, H_MEM, HD_MEM)
    p_mem_v = heads_out(mvT, H_MEM, HD_MEM)

    xs = x_sample.reshape(DB, D)
    d_in = w.shape[1]
    d_pad = -(-d_in // SAMPLE_PROJ_TILE) * SAMPLE_PROJ_TILE
    xn_s, proj = sample_proj(xs, g_attn_row, bf(jnp.pad(w, ((0, 0), (0, d_pad - d_in)))), SAMPLE_PROJ_TILE)
    parts, off = [], 0
    for n in _SPLITS[:-1]:
        parts.append(proj[:, off:off + n])
        off += n
    s_fq, s_fk, s_fv, s_ff, s_gq, s_gk, s_gv, s_ga, s_gr = parts
    lf_new = logsig_bias(s_ff, row(b_fox_f[0]))

    pool_T = lambda c: c.transpose(0, 2, 3, 1).reshape(c.shape[0], W_FOX, c.shape[1])
    fox_o_s = decode_attn(page_table, _lane_rep(s_fq), _lane_rep(s_fk), _lane_rep(s_fv), _lane_rep(lf_new),
                          pool_T(cache_fox_k[0]), pool_T(cache_fox_v[0]),
                          cache_fox_logf[0].transpose(0, 2, 1), DECODE_PAGES)
    gla_o_s, s_state = sample_gla(_lane_rep(s_gq), _lane_rep(s_gk), _lane_rep(s_ga), s_gv[:, None, :],
                                  s_gr[:, None, :], state_gla[0], bfT(w_gla_gate_up[0]), col(b_gla_gate[0]),
                                  row(g_gla_norm[0]))
    mem_T = lambda c: c.transpose(0, 2, 3, 1).reshape(c.shape[0], W_MEM, c.shape[1])
    xn_rows = jnp.broadcast_to(xn_s[:, None, :], (DB, SAMPLE_MEM_ROWS, D))
    mem_o_s = mem_attn(xn_rows, bf(w_mq), mem_T(cache_mem_k[0]), mem_T(cache_mem_v[0]), SAMPLE_MEM_ROWS)[:, 0]
    h_s = merge(xs, xn_s, bf(fox_o_s.reshape(DB, W_FOX)), bf(gla_o_s.reshape(DB, W_GLA_V)), mem_o_s,
                *merge_w, DB)
    y_sample = mlp(h_s, *mlp_w, DB).reshape(DB, 1, D)

    s_fox_k = s_fk.reshape(1, DB, 1, H_FOX, HD_FOX)
    s_fox_v = s_fv.reshape(1, DB, 1, H_FOX, HD_FOX)
    s_fox_logf = lf_new.reshape(1, DB, 1, H_FOX)
    return (y_prompt, y_sample, p_fox_k, p_fox_v, p_fox_logf, p_state[None], p_mem_k, p_mem_v,
            s_fox_k, s_fox_v, s_fox_logf, s_state[None])
```

```python
import functools

import jax
import jax.numpy as jnp
from jax import lax
from jax.experimental import pallas as pl
from jax.experimental.pallas import tpu as pltpu

F32 = jnp.float32
BF16 = jnp.bfloat16

H_FOX = 16
HD_FOX = 64
W_FOX = H_FOX * HD_FOX
H_GLA = 4
DK_GLA = 128
DV_GLA = 256
W_GLA_K = H_GLA * DK_GLA
W_GLA_V = H_GLA * DV_GLA
GLA_RANK = 16
GLA_NORMALIZER = 16.0
GLA_CHUNK = 64
MEM_LEN = 256
H_MEM = 4
HD_MEM = 64
W_MEM = H_MEM * HD_MEM
N_BRANCH = 3
EPS = 1e-6

LANES = 128
FOX_AUG = 128
VMEM_LIMIT = 56 * 1024 * 1024


def _dot(a, b):
    return jnp.dot(a, b, preferred_element_type=F32)


def _dot_nt(a, b):
    return lax.dot_general(a, b, (((1,), (1,)), ((), ())), preferred_element_type=F32)


def _split_bf16(x, parts):
    out = []
    r = x
    for _ in range(parts):
        p = r.astype(BF16)
        out.append(p)
        r = r - p.astype(F32)
    return out


def _log_sigmoid(x):
    return jnp.minimum(x, 0.0) - jnp.log1p(jnp.exp(-jnp.abs(x)))


def _sigmoid(x):
    return 1.0 / (1.0 + jnp.exp(-x))


def _rmsnorm(x, g):
    ms = jnp.mean(x * x, axis=-1, keepdims=True)
    return x * lax.rsqrt(ms + EPS) * g


def _lane_cumsum(x, reverse=False):
    rows, n = x.shape
    lane = lax.broadcasted_iota(jnp.int32, (rows, LANES), 1)
    blocks = []
    for i in range(n // LANES):
        y = x[:, i * LANES:(i + 1) * LANES]
        k = 1
        while k < LANES:
            if reverse:
                y = y + jnp.where(lane < LANES - k, pltpu.roll(y, LANES - k, 1), 0.0)
            else:
                y = y + jnp.where(lane >= k, pltpu.roll(y, k, 1), 0.0)
            k *= 2
        blocks.append(y)
    order = range(len(blocks) - 1, -1, -1) if reverse else range(len(blocks))
    edge = 0 if reverse else LANES - 1
    carry = None
    for i in order:
        if carry is not None:
            blocks[i] = blocks[i] + carry
        carry = blocks[i][:, edge:edge + 1]
    return blocks[0] if len(blocks) == 1 else jnp.concatenate(blocks, axis=1)


def _const_spec(shape):
    return pl.BlockSpec(shape, lambda *_: (0,) * len(shape))


def _params(sem):
    return pltpu.CompilerParams(dimension_semantics=sem, vmem_limit_bytes=VMEM_LIMIT)


def _split3_f32(x):
    hi = x.astype(BF16).astype(F32)
    r = x - hi
    mid = r.astype(BF16).astype(F32)
    lo = (r - mid).astype(BF16).astype(F32)
    return hi, mid, lo


def _fox_proj_kernel(x_ref, g_ref, wqT_ref, wk_ref, wvT_ref, wffT_ref, bcol_ref, e_ref,
                     xn_ref, qT_ref, ka_ref, kT_ref, vT_ref, vT16_ref, lfT_ref, cT_ref, carry_ref):
    i = pl.program_id(1)
    ts = x_ref.shape[1]

    @pl.when(i == 0)
    def _():
        carry_ref[...] = jnp.zeros_like(carry_ref)

    xn = _rmsnorm(x_ref[0], g_ref[...]).astype(BF16)
    xn_ref[0] = xn
    qT_ref[0] = (_dot_nt(wqT_ref[...], xn) * (HD_FOX ** -0.5)).astype(BF16)
    vT = _dot_nt(wvT_ref[...], xn)
    vT_ref[0] = vT
    vT16_ref[0] = vT.astype(BF16)
    k = _dot(xn, wk_ref[...])
    kT_ref[0] = k.T

    lfT = _log_sigmoid(_dot_nt(wffT_ref[...], xn) + bcol_ref[...])
    lfT_ref[0] = lfT
    cT = _lane_cumsum(lfT) + carry_ref[...]
    carry_ref[...] = cT[:, ts - 1:ts]
    cT_ref[0] = cT

    hi, mid, lo = _split3_f32(-cT)
    stack = jnp.concatenate([hi, mid, lo, jnp.ones((H_FOX, ts), F32),
                             jnp.zeros((LANES - 4 * H_FOX, ts), F32)], axis=0)
    ex = _dot(stack.T.astype(BF16), e_ref[...])
    lower = lax.broadcasted_iota(jnp.int32, (ts, LANES), 1) < HD_FOX
    for g in range(H_FOX // 2):
        kg = k[:, g * LANES:(g + 1) * LANES]
        eg = ex[:, g * LANES:(g + 1) * LANES]
        ka_ref[0, :, (2 * g) * FOX_AUG:(2 * g + 1) * FOX_AUG] = jnp.where(lower, kg, eg).astype(BF16)
        ka_ref[0, :, (2 * g + 1) * FOX_AUG:(2 * g + 2) * FOX_AUG] = jnp.where(lower, eg, kg).astype(BF16)


def _fox_bias_selector():
    e = jnp.zeros((LANES, W_FOX), F32)
    for h in range(H_FOX):
        base = (h // 2) * LANES + (HD_FOX if h % 2 == 0 else 0)
        for j in range(3):
            e = e.at[3 * H_FOX, base + j].set(1.0)
            e = e.at[j * H_FOX + h, base + 3 + j].set(1.0)
    return e.astype(BF16)


def fox_proj(x, g, wqT, wk, wvT, wffT, bcol, ts):
    B, S, D = x.shape
    e = _fox_bias_selector()
    out_shape = (
        jax.ShapeDtypeStruct((B, S, D), BF16),
        jax.ShapeDtypeStruct((B, W_FOX, S), BF16),
        jax.ShapeDtypeStruct((B, S, H_FOX * FOX_AUG), BF16),
        jax.ShapeDtypeStruct((B, W_FOX, S), F32),
        jax.ShapeDtypeStruct((B, W_FOX, S), F32),
        jax.ShapeDtypeStruct((B, W_FOX, S), BF16),
        jax.ShapeDtypeStruct((B, H_FOX, S), F32),
        jax.ShapeDtypeStruct((B, H_FOX, S), F32),
    )
    tile = lambda n: pl.BlockSpec((1, ts, n), lambda b, i: (b, i, 0))
    tileT = lambda n: pl.BlockSpec((1, n, ts), lambda b, i: (b, 0, i))
    consts = (g, wqT, wk, wvT, wffT, bcol, e)
    return pl.pallas_call(
        _fox_proj_kernel,
        grid=(B, S // ts),
        in_specs=[tile(D)] + [_const_spec(c.shape) for c in consts],
        out_specs=(tile(D), tileT(W_FOX), tile(H_FOX * FOX_AUG), tileT(W_FOX), tileT(W_FOX),
                   tileT(W_FOX), tileT(H_FOX), tileT(H_FOX)),
        out_shape=out_shape,
        scratch_shapes=[pltpu.VMEM((H_FOX, 1), F32)],
        compiler_params=_params(("parallel", "arbitrary")),
        name="fox_proj",
    )(x, *consts)


L_ROWS = 16


def _fox_attn_kernel(qi_ref, kj_ref, qT_ref, cT_ref, ka_ref, vT_ref, o_ref,
                     qa_ref, s_ref, mx_ref, m_ref, acc_ref, *, tq):
    pair = pl.program_id(1)
    S = qT_ref.shape[2]
    tk = tq
    n_pairs = qi_ref.shape[0]

    row8 = lax.broadcasted_iota(jnp.int32, (8, S), 0)
    for h in range(2):
        hi, mid, lo = _split3_f32(cT_ref[0, pl.ds(2 * pair + h, 1), :])
        ext8 = jnp.where(row8 == 0, hi, jnp.where(row8 == 1, mid, jnp.where(
            row8 == 2, lo, jnp.where(row8 < 6, 1.0, 0.0))))
        ext = jnp.concatenate([ext8, jnp.zeros((HD_FOX - 8, S), F32)], axis=0).astype(BF16)
        qh = qT_ref[0, h * HD_FOX:(h + 1) * HD_FOX, :]
        qa_ref[h] = jnp.concatenate([qh, ext] if h == 0 else [ext, qh], axis=0)
        m_ref[h] = jnp.full((1, tq), -jnp.inf, F32)
        acc_ref[h] = jnp.zeros((HD_FOX + L_ROWS, tq), F32)

    krow = lax.broadcasted_iota(jnp.int32, (tk, tq), 0)
    qcol = lax.broadcasted_iota(jnp.int32, (tk, tq), 1)
    causal = krow <= qcol
    ones = jnp.ones((L_ROWS, tk), BF16)

    def scores(t, slot, h):
        qoff = pl.multiple_of(qi_ref[t] * tq, tq)
        koff = pl.multiple_of(kj_ref[t] * tk, tk)
        sT = _dot(ka_ref[0, pl.ds(koff, tk), h * FOX_AUG:(h + 1) * FOX_AUG],
                  qa_ref[h, :, pl.ds(qoff, tq)])
        s_ref[slot, h] = sT
        mx_ref[slot, h] = jnp.max(sT, axis=0, keepdims=True)

    def consume(t, slot, h, diagonal):
        koff = pl.multiple_of(kj_ref[t] * tk, tk)
        sT = s_ref[slot, h]
        if diagonal:
            sT = jnp.where(causal, sT, -jnp.inf)
            mx = jnp.max(sT, axis=0, keepdims=True)
        else:
            mx = mx_ref[slot, h]
        m_prev = m_ref[h]
        m_next = jnp.maximum(m_prev, mx)
        pT = jnp.exp(sT - m_next).astype(BF16)
        alpha = jnp.exp(m_prev - m_next)
        va = jnp.concatenate([vT_ref[0, h * HD_FOX:(h + 1) * HD_FOX, pl.ds(koff, tk)], ones], axis=0)
        acc = alpha * acc_ref[h] + _dot(va, pT)
        if not diagonal:
            m_ref[h] = m_next
            acc_ref[h] = acc
            return None
        m_ref[h] = jnp.full((1, tq), -jnp.inf, F32)
        acc_ref[h] = jnp.zeros((HD_FOX + L_ROWS, tq), F32)
        return acc[:HD_FOX] / acc[HD_FOX:HD_FOX + 1]

    def trip(t, slot, diagonal):
        nxt = jnp.minimum(t + 1, n_pairs - 1)
        outs = []
        for h in range(2):
            scores(nxt, 1 - slot, h)
            outs.append(consume(t, slot, h, diagonal))
        if diagonal:
            qoff = pl.multiple_of(qi_ref[t] * tq, tq)
            o_ref[0, pl.ds(qoff, tq), :] = jnp.concatenate(outs, axis=0).T.astype(o_ref.dtype)

    def either(t, slot):
        is_diag = kj_ref[t] == qi_ref[t]

        @pl.when(is_diag)
        def _():
            trip(t, slot, True)

        @pl.when(jnp.logical_not(is_diag))
        def _():
            trip(t, slot, False)

    for h in range(2):
        scores(0, 0, h)

    def body(u, carry):
        either(2 * u, 0)
        either(2 * u + 1, 1)
        return carry

    lax.fori_loop(0, n_pairs // 2, body, 0)
    if n_pairs % 2:
        either(n_pairs - 1, 0)


def fox_attn(qT, cT, ka, vT16, tq):
    B, _, S = qT.shape
    nq = S // tq
    qi = jnp.asarray([i for i in range(nq) for _ in range(i + 1)], jnp.int32)
    kj = jnp.asarray([j for i in range(nq) for j in range(i + 1)], jnp.int32)
    grid_spec = pltpu.PrefetchScalarGridSpec(
        num_scalar_prefetch=2,
        grid=(B, H_FOX // 2),
        in_specs=[pl.BlockSpec((1, 2 * HD_FOX, S), lambda b, p, *_: (b, p, 0)),
                  pl.BlockSpec((1, H_FOX, S), lambda b, p, *_: (b, 0, 0)),
                  pl.BlockSpec((1, S, 2 * FOX_AUG), lambda b, p, *_: (b, 0, p)),
                  pl.BlockSpec((1, 2 * HD_FOX, S), lambda b, p, *_: (b, p, 0))],
        out_specs=pl.BlockSpec((1, S, 2 * HD_FOX), lambda b, p, *_: (b, 0, p)),
        scratch_shapes=[pltpu.VMEM((2, FOX_AUG, S), BF16),
                        pltpu.VMEM((2, 2, tq, tq), F32),
                        pltpu.VMEM((2, 2, 1, tq), F32),
                        pltpu.VMEM((2, 1, tq), F32),
                        pltpu.VMEM((2, HD_FOX + L_ROWS, tq), F32)],
    )
    return pl.pallas_call(
        functools.partial(_fox_attn_kernel, tq=tq),
        grid_spec=grid_spec,
        out_shape=jax.ShapeDtypeStruct((B, S, W_FOX), BF16),
        compiler_params=_params(("parallel", "parallel")),
        name="fox_attn",
    )(qi, kj, qT, cT, ka, vT16)


def _gla_kernel(xn_ref, wq_ref, wkT_ref, wv_ref, wa_ref, waT_ref, wr_ref, wu_ref, wuT_ref,
                brow_ref, bcol_ref, g_ref, o_ref, st_ref, s_ref, oacc_ref):
    t = pl.program_id(1)
    T = xn_ref.shape[1]
    C = GLA_CHUNK

    @pl.when(t == 0)
    def _():
        s_ref[...] = jnp.zeros_like(s_ref)

    xn = xn_ref[0]
    q = _dot(xn, wq_ref[...]) * (DK_GLA ** -0.5)
    kT = _dot_nt(wkT_ref[...], xn)
    v = _dot(xn, wv_ref[...]).astype(BF16)
    ga = _dot(xn, wa_ref[...]).astype(BF16)
    gaT = _dot_nt(waT_ref[...], xn).astype(BF16)
    la = _log_sigmoid(_dot(ga, wu_ref[...]) + brow_ref[...]) / GLA_NORMALIZER
    laT = _log_sigmoid(_dot(wuT_ref[...], gaT) + bcol_ref[...]) / GLA_NORMALIZER

    r = lax.broadcasted_iota(jnp.int32, (T, T), 0)
    c_ = lax.broadcasted_iota(jnp.int32, (T, T), 1)
    same = (r // C) == (c_ // C)
    lower = jnp.logical_and(same, c_ <= r).astype(BF16)
    upper = jnp.logical_and(same, r <= c_).astype(BF16)
    block = same.astype(BF16)
    b = bT = totT = None
    for part in _split_bf16(la, 2):
        d = _dot(lower, part)
        b = d if b is None else b + d
    for part in _split_bf16(laT, 2):
        d = _dot(part, upper)
        bT = d if bT is None else bT + d
        d = _dot(part, block)
        totT = d if totT is None else totT + d

    qe = (q * jnp.exp(b)).astype(BF16)
    kdT = (kT * jnp.exp(-bT)).astype(BF16)
    keT = (kT * jnp.exp(totT - bT)).astype(BF16)
    decT = jnp.exp(totT)

    ri = lax.broadcasted_iota(jnp.int32, (C, C), 0)
    ci = lax.broadcasted_iota(jnp.int32, (C, C), 1)
    tril = ci <= ri
    for c in range(T // C):
        rows = slice(c * C, (c + 1) * C)
        for h in range(H_GLA):
            kk = slice(h * DK_GLA, (h + 1) * DK_GLA)
            vv = slice(h * DV_GLA, (h + 1) * DV_GLA)
            qe_c = qe[rows, kk]
            v_c = v[rows, vv]
            a = jnp.where(tril, _dot(qe_c, kdT[kk, rows]), 0.0)
            s_old = s_ref[h]
            oacc_ref[rows, vv] = _dot(a.astype(BF16), v_c) + _dot(qe_c, s_old.astype(BF16))
            dec = decT[kk, c * C:c * C + 1]
            s_ref[h] = dec * s_old + _dot(keT[kk, rows], v_c)

    gr = _dot(xn, wr_ref[...])
    gate = gr * _sigmoid(gr)
    for h in range(H_GLA):
        vv = slice(h * DV_GLA, (h + 1) * DV_GLA)
        y = _rmsnorm(oacc_ref[:, vv], g_ref[...])
        o_ref[0, :, vv] = (y * gate[:, vv]).astype(o_ref.dtype)

    @pl.when(t == pl.num_programs(1) - 1)
    def _():
        st_ref[0] = s_ref[...]


def gla(xn, wq, wkT, wv, wa, waT, wr, wu, wuT, brow, bcol, g, T):
    B, S, D = xn.shape
    consts = (wq, wkT, wv, wa, waT, wr, wu, wuT, brow, bcol, g)
    return pl.pallas_call(
        _gla_kernel,
        grid=(B, S // T),
        in_specs=[pl.BlockSpec((1, T, D), lambda b, t: (b, t, 0))] + [_const_spec(w.shape) for w in consts],
        out_specs=(pl.BlockSpec((1, T, W_GLA_V), lambda b, t: (b, t, 0)),
                   pl.BlockSpec((1, H_GLA, DK_GLA, DV_GLA), lambda b, t: (b, 0, 0, 0))),
        out_shape=(jax.ShapeDtypeStruct((B, S, W_GLA_V), BF16),
                   jax.ShapeDtypeStruct((B, H_GLA, DK_GLA, DV_GLA), F32)),
        scratch_shapes=[pltpu.VMEM((H_GLA, DK_GLA, DV_GLA), F32), pltpu.VMEM((T, W_GLA_V), F32)],
        compiler_params=_params(("parallel", "arbitrary")),
        name="gla",
    )(xn, *consts)


def _mem_kv_kernel(m_ref, g_ref, wkT_ref, wvT_ref, kT_ref, vT_ref):
    mn = _rmsnorm(m_ref[0], g_ref[...]).astype(BF16)
    kT_ref[0] = _dot_nt(wkT_ref[...], mn)
    vT_ref[0] = _dot_nt(wvT_ref[...], mn)


def mem_kv(mem, g, wkT, wvT):
    B, M, D = mem.shape
    blk = pl.BlockSpec((1, W_MEM, M), lambda b: (b, 0, 0))
    return pl.pallas_call(
        _mem_kv_kernel,
        grid=(B,),
        in_specs=[pl.BlockSpec((1, M, D), lambda b: (b, 0, 0)), _const_spec(g.shape),
                  _const_spec(wkT.shape), _const_spec(wvT.shape)],
        out_specs=(blk, blk),
        out_shape=(jax.ShapeDtypeStruct((B, W_MEM, M), F32),) * 2,
        compiler_params=_params(("parallel",)),
        name="mem_kv",
    )(mem, g, wkT, wvT)


def _mem_attn_kernel(xn_ref, wq_ref, kT_ref, vT_ref, o_ref):
    q = _dot(xn_ref[0], wq_ref[...]).astype(BF16)
    outs = []
    for h in range(H_MEM):
        hh = slice(h * HD_MEM, (h + 1) * HD_MEM)
        s = _dot(q[:, hh], kT_ref[0, hh, :].astype(BF16)) * (HD_MEM ** -0.5)
        p = jnp.exp(s - jnp.max(s, axis=1, keepdims=True))
        p = p / jnp.sum(p, axis=1, keepdims=True)
        outs.append(_dot_nt(p.astype(BF16), vT_ref[0, hh, :].astype(BF16)))
    o_ref[0] = jnp.concatenate(outs, axis=1).astype(o_ref.dtype)


def mem_attn(xn, wq, kT, vT, tl):
    B, L, D = xn.shape
    M = kT.shape[2]
    return pl.pallas_call(
        _mem_attn_kernel,
        grid=(B, L // tl),
        in_specs=[pl.BlockSpec((1, tl, D), lambda b, i: (b, i, 0)), _const_spec(wq.shape),
                  pl.BlockSpec((1, W_MEM, M), lambda b, i: (b, 0, 0)),
                  pl.BlockSpec((1, W_MEM, M), lambda b, i: (b, 0, 0))],
        out_specs=pl.BlockSpec((1, tl, W_MEM), lambda b, i: (b, i, 0)),
        out_shape=jax.ShapeDtypeStruct((B, L, W_MEM), BF16),
        compiler_params=_params(("parallel", "parallel")),
        name="mem_attn",
    )(xn, wq, kT, vT)


def _merge_kernel(x_ref, xn_ref, fo_ref, go_ref, mo_ref, wgl_ref, wuf_ref, wug_ref, wum_ref, wout_ref, h_ref):
    D = x_ref.shape[1]
    xn = xn_ref[...]
    merged = None
    for k, (br, w) in enumerate(((fo_ref, wuf_ref), (go_ref, wug_ref), (mo_ref, wum_ref))):
        gate = _sigmoid(_dot(xn, wgl_ref[:, k * D:(k + 1) * D]))
        t = gate * _dot(br[...], w[...])
        merged = t if merged is None else merged + t
    h_ref[...] = x_ref[...] + _dot(merged.astype(BF16), wout_ref[...])


def merge(x, xn, fo, go, mo, wgl, wuf, wug, wum, wout, tm):
    N, D = x.shape
    row = lambda n: pl.BlockSpec((tm, n), lambda i: (i, 0))
    consts = (wgl, wuf, wug, wum, wout)
    return pl.pallas_call(
        _merge_kernel,
        grid=(N // tm,),
        in_specs=[row(D), row(D), row(fo.shape[1]), row(go.shape[1]), row(mo.shape[1])]
                 + [_const_spec(w.shape) for w in consts],
        out_specs=row(D),
        out_shape=jax.ShapeDtypeStruct((N, D), F32),
        compiler_params=_params(("parallel",)),
        name="merge",
    )(x, xn, fo, go, mo, *consts)


FF_CHUNK = 1024


def _mlp_kernel(h_ref, g_ref, w1_ref, w2_ref, gf_ref, y_ref):
    h = h_ref[...]
    hn = _rmsnorm(h, g_ref[...]).astype(BF16)
    acc = h
    for c in range(w1_ref.shape[1] // FF_CHUNK):
        cc = slice(c * FF_CHUNK, (c + 1) * FF_CHUNK)
        u = jnp.maximum(_dot(hn, w1_ref[:, cc]), 0.0)
        acc = acc + _dot((u * u).astype(BF16), w2_ref[cc, :])
    y_ref[...] = _rmsnorm(acc, gf_ref[...])


def mlp(h, g, w1, w2, gf, tm):
    N, D = h.shape
    row = pl.BlockSpec((tm, D), lambda i: (i, 0))
    return pl.pallas_call(
        _mlp_kernel,
        grid=(N // tm,),
        in_specs=[row, _const_spec(g.shape), _const_spec(w1.shape), _const_spec(w2.shape),
                  _const_spec(gf.shape)],
        out_specs=row,
        out_shape=jax.ShapeDtypeStruct((N, D), F32),
        compiler_params=_params(("parallel",)),
        name="mlp",
    )(h, g, w1, w2, gf)


def _sample_proj_kernel(x_ref, g_ref, w_ref, xn_ref, o_ref):
    xn = _rmsnorm(x_ref[...], g_ref[...]).astype(BF16)
    xn_ref[...] = xn
    o_ref[...] = _dot(xn, w_ref[...])


def sample_proj(x, g, w, tn):
    N, D = x.shape
    NP = w.shape[1]
    return pl.pallas_call(
        _sample_proj_kernel,
        grid=(NP // tn,),
        in_specs=[_const_spec((N, D)), _const_spec(g.shape), pl.BlockSpec((D, tn), lambda j: (0, j))],
        out_specs=(_const_spec((N, D)), pl.BlockSpec((N, tn), lambda j: (0, j))),
        out_shape=(jax.ShapeDtypeStruct((N, D), BF16), jax.ShapeDtypeStruct((N, NP), F32)),
        compiler_params=_params(("arbitrary",)),
        name="sample_proj",
    )(x, g, w)


def _logsig_kernel(ff_ref, b_ref, o_ref):
    o_ref[...] = _log_sigmoid(ff_ref[...] + b_ref[...])


def logsig_bias(ff, b):
    return pl.pallas_call(_logsig_kernel, out_shape=jax.ShapeDtypeStruct(ff.shape, F32), name="logsig")(ff, b)


def _decode_kernel(pt_ref, q_ref, kn_ref, vn_ref, lfn_ref, *refs, pages):
    k_refs, v_refs, lf_refs = refs[:pages], refs[pages:2 * pages], refs[2 * pages:3 * pages]
    o_ref, qs_ref, m_ref, l_ref, r_ref, acc_ref = refs[3 * pages:]
    j = pl.program_id(1)
    lane = lax.broadcasted_iota(jnp.int32, (H_FOX, LANES), 1)
    lane_w = lax.broadcasted_iota(jnp.int32, (W_FOX, LANES), 1)

    def head_scores(k_ref):
        return jnp.concatenate(
            [jnp.sum(k_ref[0, h * HD_FOX:(h + 1) * HD_FOX, :] * qs_ref[h * HD_FOX:(h + 1) * HD_FOX, :],
                     axis=0, keepdims=True) for h in range(H_FOX)], axis=0)

    @pl.when(j == 0)
    def _():
        qs_ref[...] = q_ref[0] * (HD_FOX ** -0.5)
        m_ref[...] = head_scores(kn_ref)
        l_ref[...] = jnp.where(lane == 0, 1.0, 0.0)
        r_ref[...] = jnp.zeros_like(r_ref)
        acc_ref[...] = jnp.where(lane_w == 0, vn_ref[0], 0.0)

    r_run = r_ref[...]
    scores = []
    for g in range(pages):
        lf = lf_refs[g][0]
        incl = _lane_cumsum(lf, reverse=True)
        scores.append(head_scores(k_refs[g]) + lfn_ref[0] + (incl - lf + r_run))
        r_run = r_run + incl[:, 0:1]
    r_ref[...] = r_run
    m_prev = m_ref[...]
    m_new = m_prev
    for s in scores:
        m_new = jnp.maximum(m_new, jnp.max(s, axis=1, keepdims=True))
    alpha = jnp.exp(m_prev - m_new)
    probs = [jnp.exp(s - m_new) for s in scores]
    l_ref[...] = alpha * l_ref[...] + functools.reduce(lambda a, b: a + b, probs)
    m_ref[...] = m_new
    for h in range(H_FOX):
        rows = slice(h * HD_FOX, (h + 1) * HD_FOX)
        a = acc_ref[rows, :] * alpha[h:h + 1, :]
        for g in range(pages):
            a = a + v_refs[g][0, rows, :] * probs[g][h:h + 1, :]
        acc_ref[rows, :] = a

    @pl.when(j == pl.num_programs(1) - 1)
    def _():
        inv = 1.0 / jnp.sum(l_ref[...], axis=1, keepdims=True)
        for h in range(H_FOX):
            rows = slice(h * HD_FOX, (h + 1) * HD_FOX)
            acc_ref[rows, :] = acc_ref[rows, :] * inv[h:h + 1, :]
        ones = jnp.ones((8, LANES), BF16)
        out = None
        for part in _split_bf16(acc_ref[...], 3):
            t = _dot_nt(ones, part)
            out = t if out is None else out + t
        o_ref[0] = out[0:1, :]


def decode_attn(page_table, q_rep, kn_rep, vn_rep, lfn_rep, kT_pool, vT_pool, lfT_pool, pages):
    DB = q_rep.shape[0]
    P = page_table.shape[1]
    per_b = lambda n: pl.BlockSpec((1, n, LANES), lambda b, j, pt: (b, 0, 0))

    def paged(n, g):
        return pl.BlockSpec((1, n, LANES), lambda b, j, pt: (pt[b, P - 1 - (j * pages + g)], 0, 0))

    grid_spec = pltpu.PrefetchScalarGridSpec(
        num_scalar_prefetch=1,
        grid=(DB, P // pages),
        in_specs=[per_b(W_FOX), per_b(W_FOX), per_b(W_FOX), per_b(H_FOX)]
                 + [paged(W_FOX, g) for g in range(pages)]
                 + [paged(W_FOX, g) for g in range(pages)]
                 + [paged(H_FOX, g) for g in range(pages)],
        out_specs=pl.BlockSpec((1, 1, W_FOX), lambda b, j, pt: (b, 0, 0)),
        scratch_shapes=[pltpu.VMEM((W_FOX, LANES), F32), pltpu.VMEM((H_FOX, LANES), F32),
                        pltpu.VMEM((H_FOX, LANES), F32), pltpu.VMEM((H_FOX, LANES), F32),
                        pltpu.VMEM((W_FOX, LANES), F32)],
    )
    return pl.pallas_call(
        functools.partial(_decode_kernel, pages=pages),
        grid_spec=grid_spec,
        out_shape=jax.ShapeDtypeStruct((DB, 1, W_FOX), F32),
        compiler_params=_params(("parallel", "arbitrary")),
        name="decode_attn",
    )(page_table, q_rep, kn_rep, vn_rep, lfn_rep, *([kT_pool] * pages), *([vT_pool] * pages),
      *([lfT_pool] * pages))


def _sample_gla_kernel(q_ref, k_ref, ga_ref, v_ref, gr_ref, s0_ref, wuT_ref, bcol_ref, g_ref, o_ref, st_ref):
    z = _dot(wuT_ref[...], ga_ref[0].astype(BF16)) + bcol_ref[...]
    la = _log_sigmoid(z) / GLA_NORMALIZER
    ea = jnp.exp(la)
    k = k_ref[0]
    qe = q_ref[0] * (DK_GLA ** -0.5) * ea
    kd = k * jnp.exp(-la)
    gr = gr_ref[0]
    gate = gr * _sigmoid(gr)
    wide = lambda a: jnp.concatenate([a, a], axis=1)
    for h in range(H_GLA):
        kk = slice(h * DK_GLA, (h + 1) * DK_GLA)
        vv = slice(h * DV_GLA, (h + 1) * DV_GLA)
        s_old = s0_ref[0, h]
        v_h = v_ref[0][:, vv]
        a = jnp.sum(qe[kk] * kd[kk], axis=0, keepdims=True)
        o = wide(a) * v_h + jnp.sum(wide(qe[kk]) * s_old, axis=0, keepdims=True)
        st_ref[0, h] = wide(ea[kk]) * s_old + wide(k[kk]) * v_h
        o_ref[0, :, vv] = _rmsnorm(o, g_ref[...]) * gate[:, vv]


def sample_gla(q_rep, k_rep, ga_rep, v, gr, s0, wuT, bcol, g):
    DB = q_rep.shape[0]
    per_b = lambda a: pl.BlockSpec((1,) + a.shape[1:], lambda b: (b,) + (0,) * (a.ndim - 1))
    return pl.pallas_call(
        _sample_gla_kernel,
        grid=(DB,),
        in_specs=[per_b(q_rep), per_b(k_rep), per_b(ga_rep), per_b(v), per_b(gr), per_b(s0),
                  _const_spec(wuT.shape), _const_spec(bcol.shape), _const_spec(g.shape)],
        out_specs=(per_b(v), per_b(s0)),
        out_shape=(jax.ShapeDtypeStruct(v.shape, F32), jax.ShapeDtypeStruct(s0.shape, F32)),
        compiler_params=_params(("parallel",)),
        name="sample_gla",
    )(q_rep, k_rep, ga_rep, v, gr, s0, wuT, bcol, g)


_SPLITS = (W_FOX, W_FOX, W_FOX, H_FOX, W_GLA_K, W_GLA_K, W_GLA_V, GLA_RANK, W_GLA_V, W_MEM)
PROMPT_TILE = 512
GLA_TILE = 256
DECODE_PAGES = 8
SAMPLE_PROJ_TILE = 512
SAMPLE_MEM_ROWS = 8


def _lane_rep(a):
    return jnp.broadcast_to(a[..., None], a.shape + (LANES,))


def kernel(x_prompt, x_sample, mem_prompt, cache_fox_k, cache_fox_v, cache_fox_logf, state_gla, cache_mem_k, cache_mem_v, page_table, g_attn, w_in, b_fox_f, w_gla_gate_up, b_gla_gate, g_gla_norm, g_mem, w_mem_k, w_mem_v, w_up_fox, w_up_gla, w_up_mem, w_out, g_mlp, w_mlp_in, w_mlp_out, g_final):
    B, S, D = x_prompt.shape
    DB = x_sample.shape[0]
    assert w_in.shape[0] == 1 and x_sample.shape[1] == 1

    w = w_in[0]
    segs, off = [], 0
    for n in _SPLITS:
        segs.append(w[:, off:off + n])
        off += n
    w_fq, w_fk, w_fv, w_ff, w_gq, w_gk, w_gv, w_ga, w_gr, w_mq = segs
    w_gl = w[:, off:]
    bf = lambda a: a.astype(BF16)
    bfT = lambda a: a.T.astype(BF16)
    row = lambda a: a.reshape(1, -1)
    col = lambda a: a.reshape(-1, 1)

    ga_args = (bf(w_gq), bfT(w_gk), bf(w_gv), bf(w_ga), bfT(w_ga), bf(w_gr), bf(w_gla_gate_up[0]),
               bfT(w_gla_gate_up[0]), row(b_gla_gate[0]), col(b_gla_gate[0]), row(g_gla_norm[0]))
    merge_w = (bf(w_gl), bf(w_up_fox[0]), bf(w_up_gla[0]), bf(w_up_mem[0]), bf(w_out[0]))
    mlp_w = (row(g_mlp[0]), bf(w_mlp_in[0]), bf(w_mlp_out[0]), row(g_final))
    g_attn_row = row(g_attn[0])

    xn, qT, ka, kT, vT, vT16, lfT, cT = fox_proj(
        x_prompt, g_attn_row, bfT(w_fq), bf(w_fk), bfT(w_fv), bfT(w_ff), col(b_fox_f[0]), PROMPT_TILE)
    fox_o = fox_attn(qT, cT, ka, vT16, PROMPT_TILE)
    gla_o, p_state = gla(xn, *ga_args, GLA_TILE)
    mkT, mvT = mem_kv(mem_prompt, row(g_mem[0]), bfT(w_mem_k[0]), bfT(w_mem_v[0]))
    mem_o = mem_attn(xn, bf(w_mq), mkT, mvT, PROMPT_TILE)
    N = B * S
    h = merge(x_prompt.reshape(N, D), xn.reshape(N, D), fox_o.reshape(N, W_FOX), gla_o.reshape(N, W_GLA_V),
              mem_o.reshape(N, W_MEM), *merge_w, PROMPT_TILE)
    y_prompt = mlp(h, *mlp_w, PROMPT_TILE).reshape(B, S, D)

    heads_out = lambda aT, nh, hd: aT.reshape(aT.shape[0], nh, hd, aT.shape[2]).transpose(0, 3, 1, 2)[None]
    p_fox_k = heads_out(kT, H_FOX, HD_FOX)
    p_fox_v = heads_out(vT, H_FOX, HD_FOX)
    p_fox_logf = lfT.transpose(0, 2, 1)[None]
    p_mem_k = heads_out(mkT, H_MEM, HD_MEM)
    p_mem_v = heads_out(mvT, H_MEM, HD_MEM)

    xs = x_sample.reshape(DB, D)
    d_in = w.shape[1]
    d_pad = -(-d_in // SAMPLE_PROJ_TILE) * SAMPLE_PROJ_TILE
    xn_s, proj = sample_proj(xs, g_attn_row, bf(jnp.pad(w, ((0, 0), (0, d_pad - d_in)))), SAMPLE_PROJ_TILE)
    parts, off = [], 0
    for n in _SPLITS[:-1]:
        parts.append(proj[:, off:off + n])
        off += n
    s_fq, s_fk, s_fv, s_ff, s_gq, s_gk, s_gv, s_ga, s_gr = parts
    lf_new = logsig_bias(s_ff, row(b_fox_f[0]))

    pool_T = lambda c: c.transpose(0, 2, 3, 1).reshape(c.shape[0], W_FOX, c.shape[1])
    fox_o_s = decode_attn(page_table, _lane_rep(s_fq), _lane_rep(s_fk), _lane_rep(s_fv), _lane_rep(lf_new),
                          pool_T(cache_fox_k[0]), pool_T(cache_fox_v[0]),
                          cache_fox_logf[0].transpose(0, 2, 1), DECODE_PAGES)
    gla_o_s, s_state = sample_gla(_lane_rep(s_gq), _lane_rep(s_gk), _lane_rep(s_ga), s_gv[:, None, :],
                                  s_gr[:, None, :], state_gla[0], bfT(w_gla_gate_up[0]), col(b_gla_gate[0]),
                                  row(g_gla_norm[0]))
    mem_T = lambda c: c.transpose(0, 2, 3, 1).reshape(c.shape[0], W_MEM, c.shape[1])
    xn_rows = jnp.broadcast_to(xn_s[:, None, :], (DB, SAMPLE_MEM_ROWS, D))
    mem_o_s = mem_attn(xn_rows, bf(w_mq), mem_T(cache_mem_k[0]), mem_T(cache_mem_v[0]), SAMPLE_MEM_ROWS)[:, 0]
    h_s = merge(xs, xn_s, bf(fox_o_s.reshape(DB, W_FOX)), bf(gla_o_s.reshape(DB, W_GLA_V)), mem_o_s,
                *merge_w, DB)
    y_sample = mlp(h_s, *mlp_w, DB).reshape(DB, 1, D)

    s_fox_k = s_fk.reshape(1, DB, 1, H_FOX, HD_FOX)
    s_fox_v = s_fv.reshape(1, DB, 1, H_FOX, HD_FOX)
    s_fox_logf = lf_new.reshape(1, DB, 1, H_FOX)
    return (y_prompt, y_sample, p_fox_k, p_fox_v, p_fox_logf, p_state[None], p_mem_k, p_mem_v,
            s_fox_k, s_fox_v, s_fox_logf, s_state[None])
```

```python
import functools
from typing import NamedTuple

import jax
import jax.numpy as jnp
from jax import lax
from jax.experimental import pallas as pl
from jax.experimental.pallas import tpu as pltpu

F32 = jnp.float32
BF16 = jnp.bfloat16

H_FOX = 16
HD_FOX = 64
W_FOX = H_FOX * HD_FOX
H_GLA = 4
DK_GLA = 128
DV_GLA = 256
W_GLA_K = H_GLA * DK_GLA
W_GLA_V = H_GLA * DV_GLA
GLA_RANK = 16
GLA_NORMALIZER = 16.0
GLA_CHUNK = 64
MEM_LEN = 256
H_MEM = 4
HD_MEM = 64
W_MEM = H_MEM * HD_MEM
N_BRANCH = 3
EPS = 1e-6

LANES = 128
FOX_AUG = 128
VMEM_LIMIT = 56 * 1024 * 1024


def _dot(a, b):
    return jnp.dot(a, b, preferred_element_type=F32)


def _dot_nt(a, b):
    return lax.dot_general(a, b, (((1,), (1,)), ((), ())), preferred_element_type=F32)


def _split_bf16(x, parts):
    out = []
    r = x
    for _ in range(parts):
        p = r.astype(BF16)
        out.append(p)
        r = r - p.astype(F32)
    return out


def _log_sigmoid(x):
    return jnp.minimum(x, 0.0) - jnp.log1p(jnp.exp(-jnp.abs(x)))


def _sigmoid(x):
    return 1.0 / (1.0 + jnp.exp(-x))


def _rmsnorm(x, g):
    ms = jnp.mean(x * x, axis=-1, keepdims=True)
    return x * lax.rsqrt(ms + EPS) * g


def _lane_cumsum(x, reverse=False):
    rows, n = x.shape
    lane = lax.broadcasted_iota(jnp.int32, (rows, LANES), 1)
    blocks = []
    for i in range(n // LANES):
        y = x[:, i * LANES:(i + 1) * LANES]
        k = 1
        while k < LANES:
            if reverse:
                y = y + jnp.where(lane < LANES - k, pltpu.roll(y, LANES - k, 1), 0.0)
            else:
                y = y + jnp.where(lane >= k, pltpu.roll(y, k, 1), 0.0)
            k *= 2
        blocks.append(y)
    order = range(len(blocks) - 1, -1, -1) if reverse else range(len(blocks))
    edge = 0 if reverse else LANES - 1
    carry = None
    for i in order:
        if carry is not None:
            blocks[i] = blocks[i] + carry
        carry = blocks[i][:, edge:edge + 1]
    return blocks[0] if len(blocks) == 1 else jnp.concatenate(blocks, axis=1)


def _const_spec(shape):
    return pl.BlockSpec(shape, lambda *_: (0,) * len(shape))


def _params(sem):
    return pltpu.CompilerParams(dimension_semantics=sem, vmem_limit_bytes=VMEM_LIMIT)


def _split3_f32(x):
    hi = x.astype(BF16).astype(F32)
    r = x - hi
    mid = r.astype(BF16).astype(F32)
    lo = (r - mid).astype(BF16).astype(F32)
    return hi, mid, lo


def _fox_proj_kernel(x_ref, g_ref, wqT_ref, wk_ref, wvT_ref, wffT_ref, bcol_ref, e_ref,
                     xn_ref, qT_ref, ka_ref, kT_ref, vT_ref, vT16_ref, lfT_ref, cT_ref, carry_ref):
    i = pl.program_id(1)
    ts = x_ref.shape[1]

    @pl.when(i == 0)
    def _():
        carry_ref[...] = jnp.zeros_like(carry_ref)

    xn = _rmsnorm(x_ref[0], g_ref[...]).astype(BF16)
    xn_ref[0] = xn
    qT_ref[0] = (_dot_nt(wqT_ref[...], xn) * (HD_FOX ** -0.5)).astype(BF16)
    vT = _dot_nt(wvT_ref[...], xn)
    vT_ref[0] = vT
    vT16_ref[0] = vT.astype(BF16)
    k = _dot(xn, wk_ref[...])
    kT_ref[0] = k.T

    lfT = _log_sigmoid(_dot_nt(wffT_ref[...], xn) + bcol_ref[...])
    lfT_ref[0] = lfT
    cT = _lane_cumsum(lfT) + carry_ref[...]
    carry_ref[...] = cT[:, ts - 1:ts]
    cT_ref[0] = cT

    hi, mid, lo = _split3_f32(-cT)
    stack = jnp.concatenate([hi, mid, lo, jnp.ones((H_FOX, ts), F32),
                             jnp.zeros((LANES - 4 * H_FOX, ts), F32)], axis=0)
    ex = _dot(stack.T.astype(BF16), e_ref[...])
    lower = lax.broadcasted_iota(jnp.int32, (ts, LANES), 1) < HD_FOX
    for g in range(H_FOX // 2):
        kg = k[:, g * LANES:(g + 1) * LANES]
        eg = ex[:, g * LANES:(g + 1) * LANES]
        ka_ref[0, :, (2 * g) * FOX_AUG:(2 * g + 1) * FOX_AUG] = jnp.where(lower, kg, eg).astype(BF16)
        ka_ref[0, :, (2 * g + 1) * FOX_AUG:(2 * g + 2) * FOX_AUG] = jnp.where(lower, eg, kg).astype(BF16)


def _fox_bias_selector():
    e = jnp.zeros((LANES, W_FOX), F32)
    for h in range(H_FOX):
        base = (h // 2) * LANES + (HD_FOX if h % 2 == 0 else 0)
        for j in range(3):
            e = e.at[3 * H_FOX, base + j].set(1.0)
            e = e.at[j * H_FOX + h, base + 3 + j].set(1.0)
    return e.astype(BF16)


def fox_proj(x, g, wqT, wk, wvT, wffT, bcol, ts):
    B, S, D = x.shape
    e = _fox_bias_selector()
    out_shape = (
        jax.ShapeDtypeStruct((B, S, D), BF16),
        jax.ShapeDtypeStruct((B, W_FOX, S), BF16),
        jax.ShapeDtypeStruct((B, S, H_FOX * FOX_AUG), BF16),
        jax.ShapeDtypeStruct((B, W_FOX, S), F32),
        jax.ShapeDtypeStruct((B, W_FOX, S), F32),
        jax.ShapeDtypeStruct((B, W_FOX, S), BF16),
        jax.ShapeDtypeStruct((B, H_FOX, S), F32),
        jax.ShapeDtypeStruct((B, H_FOX, S), F32),
    )
    tile = lambda n: pl.BlockSpec((1, ts, n), lambda b, i: (b, i, 0))
    tileT = lambda n: pl.BlockSpec((1, n, ts), lambda b, i: (b, 0, i))
    consts = (g, wqT, wk, wvT, wffT, bcol, e)
    return pl.pallas_call(
        _fox_proj_kernel,
        grid=(B, S // ts),
        in_specs=[tile(D)] + [_const_spec(c.shape) for c in consts],
        out_specs=(tile(D), tileT(W_FOX), tile(H_FOX * FOX_AUG), tileT(W_FOX), tileT(W_FOX),
                   tileT(W_FOX), tileT(H_FOX), tileT(H_FOX)),
        out_shape=out_shape,
        scratch_shapes=[pltpu.VMEM((H_FOX, 1), F32)],
        compiler_params=_params(("parallel", "arbitrary")),
        name="fox_proj",
    )(x, *consts)


L_ROWS = 16


def _fox_attn_kernel(qi_ref, kj_ref, qT_ref, cT_ref, ka_ref, vT_ref, o_ref,
                     qa_ref, s_ref, mx_ref, m_ref, acc_ref, *, tq):
    pair = pl.program_id(1)
    S = qT_ref.shape[2]
    tk = tq
    n_pairs = qi_ref.shape[0]

    row8 = lax.broadcasted_iota(jnp.int32, (8, S), 0)
    for h in range(2):
        hi, mid, lo = _split3_f32(cT_ref[0, pl.ds(2 * pair + h, 1), :])
        ext8 = jnp.where(row8 == 0, hi, jnp.where(row8 == 1, mid, jnp.where(
            row8 == 2, lo, jnp.where(row8 < 6, 1.0, 0.0))))
        ext = jnp.concatenate([ext8, jnp.zeros((HD_FOX - 8, S), F32)], axis=0).astype(BF16)
        qh = qT_ref[0, h * HD_FOX:(h + 1) * HD_FOX, :]
        qa_ref[h] = jnp.concatenate([qh, ext] if h == 0 else [ext, qh], axis=0)
        m_ref[h] = jnp.full((1, tq), -jnp.inf, F32)
        acc_ref[h] = jnp.zeros((HD_FOX + L_ROWS, tq), F32)

    krow = lax.broadcasted_iota(jnp.int32, (tk, tq), 0)
    qcol = lax.broadcasted_iota(jnp.int32, (tk, tq), 1)
    causal = krow <= qcol
    ones = jnp.ones((L_ROWS, tk), BF16)

    def scores(t, slot, h):
        qoff = pl.multiple_of(qi_ref[t] * tq, tq)
        koff = pl.multiple_of(kj_ref[t] * tk, tk)
        sT = _dot(ka_ref[0, pl.ds(koff, tk), h * FOX_AUG:(h + 1) * FOX_AUG],
                  qa_ref[h, :, pl.ds(qoff, tq)])
        s_ref[slot, h] = sT
        mx_ref[slot, h] = jnp.max(sT, axis=0, keepdims=True)

    def consume(t, slot, h, diagonal):
        koff = pl.multiple_of(kj_ref[t] * tk, tk)
        sT = s_ref[slot, h]
        if diagonal:
            sT = jnp.where(causal, sT, -jnp.inf)
            mx = jnp.max(sT, axis=0, keepdims=True)
        else:
            mx = mx_ref[slot, h]
        m_prev = m_ref[h]
        m_next = jnp.maximum(m_prev, mx)
        pT = jnp.exp(sT - m_next).astype(BF16)
        alpha = jnp.exp(m_prev - m_next)
        va = jnp.concatenate([vT_ref[0, h * HD_FOX:(h + 1) * HD_FOX, pl.ds(koff, tk)], ones], axis=0)
        acc = alpha * acc_ref[h] + _dot(va, pT)
        if not diagonal:
            m_ref[h] = m_next
            acc_ref[h] = acc
            return None
        m_ref[h] = jnp.full((1, tq), -jnp.inf, F32)
        acc_ref[h] = jnp.zeros((HD_FOX + L_ROWS, tq), F32)
        return acc[:HD_FOX] / acc[HD_FOX:HD_FOX + 1]

    def trip(t, slot, diagonal):
        nxt = jnp.minimum(t + 1, n_pairs - 1)
        outs = []
        for h in range(2):
            scores(nxt, 1 - slot, h)
            outs.append(consume(t, slot, h, diagonal))
        if diagonal:
            qoff = pl.multiple_of(qi_ref[t] * tq, tq)
            o_ref[0, pl.ds(qoff, tq), :] = jnp.concatenate(outs, axis=0).T.astype(o_ref.dtype)

    def either(t, slot):
        is_diag = kj_ref[t] == qi_ref[t]

        @pl.when(is_diag)
        def _():
            trip(t, slot, True)

        @pl.when(jnp.logical_not(is_diag))
        def _():
            trip(t, slot, False)

    for h in range(2):
        scores(0, 0, h)

    def body(u, carry):
        either(2 * u, 0)
        either(2 * u + 1, 1)
        return carry

    lax.fori_loop(0, n_pairs // 2, body, 0)
    if n_pairs % 2:
        either(n_pairs - 1, 0)


def fox_attn(qT, cT, ka, vT16, tq):
    B, _, S = qT.shape
    nq = S // tq
    qi = jnp.asarray([i for i in range(nq) for _ in range(i + 1)], jnp.int32)
    kj = jnp.asarray([j for i in range(nq) for j in range(i + 1)], jnp.int32)
    grid_spec = pltpu.PrefetchScalarGridSpec(
        num_scalar_prefetch=2,
        grid=(B, H_FOX // 2),
        in_specs=[pl.BlockSpec((1, 2 * HD_FOX, S), lambda b, p, *_: (b, p, 0)),
                  pl.BlockSpec((1, H_FOX, S), lambda b, p, *_: (b, 0, 0)),
                  pl.BlockSpec((1, S, 2 * FOX_AUG), lambda b, p, *_: (b, 0, p)),
                  pl.BlockSpec((1, 2 * HD_FOX, S), lambda b, p, *_: (b, p, 0))],
        out_specs=pl.BlockSpec((1, S, 2 * HD_FOX), lambda b, p, *_: (b, 0, p)),
        scratch_shapes=[pltpu.VMEM((2, FOX_AUG, S), BF16),
                        pltpu.VMEM((2, 2, tq, tq), F32),
                        pltpu.VMEM((2, 2, 1, tq), F32),
                        pltpu.VMEM((2, 1, tq), F32),
                        pltpu.VMEM((2, HD_FOX + L_ROWS, tq), F32)],
    )
    return pl.pallas_call(
        functools.partial(_fox_attn_kernel, tq=tq),
        grid_spec=grid_spec,
        out_shape=jax.ShapeDtypeStruct((B, S, W_FOX), BF16),
        compiler_params=_params(("parallel", "parallel")),
        name="fox_attn",
    )(qi, kj, qT, cT, ka, vT16)


def _gla_kernel(xn_ref, wq_ref, wkT_ref, wv_ref, wa_ref, waT_ref, wr_ref, wu_ref, wuT_ref,
                brow_ref, bcol_ref, g_ref, o_ref, st_ref, s_ref, oacc_ref):
    t = pl.program_id(1)
    T = xn_ref.shape[1]
    C = GLA_CHUNK

    @pl.when(t == 0)
    def _():
        s_ref[...] = jnp.zeros_like(s_ref)

    xn = xn_ref[0]
    q = _dot(xn, wq_ref[...]) * (DK_GLA ** -0.5)
    kT = _dot_nt(wkT_ref[...], xn)
    v = _dot(xn, wv_ref[...]).astype(BF16)
    ga = _dot(xn, wa_ref[...]).astype(BF16)
    gaT = _dot_nt(waT_ref[...], xn).astype(BF16)
    la = _log_sigmoid(_dot(ga, wu_ref[...]) + brow_ref[...]) / GLA_NORMALIZER
    laT = _log_sigmoid(_dot(wuT_ref[...], gaT) + bcol_ref[...]) / GLA_NORMALIZER

    r = lax.broadcasted_iota(jnp.int32, (T, T), 0)
    c_ = lax.broadcasted_iota(jnp.int32, (T, T), 1)
    same = (r // C) == (c_ // C)
    lower = jnp.logical_and(same, c_ <= r).astype(BF16)
    upper = jnp.logical_and(same, r <= c_).astype(BF16)
    block = same.astype(BF16)
    b = bT = totT = None
    for part in _split_bf16(la, 2):
        d = _dot(lower, part)
        b = d if b is None else b + d
    for part in _split_bf16(laT, 2):
        d = _dot(part, upper)
        bT = d if bT is None else bT + d
        d = _dot(part, block)
        totT = d if totT is None else totT + d

    qe = (q * jnp.exp(b)).astype(BF16)
    kdT = (kT * jnp.exp(-bT)).astype(BF16)
    keT = (kT * jnp.exp(totT - bT)).astype(BF16)
    decT = jnp.exp(totT)

    ri = lax.broadcasted_iota(jnp.int32, (C, C), 0)
    ci = lax.broadcasted_iota(jnp.int32, (C, C), 1)
    tril = ci <= ri
    for c in range(T // C):
        rows = slice(c * C, (c + 1) * C)
        for h in range(H_GLA):
            kk = slice(h * DK_GLA, (h + 1) * DK_GLA)
            vv = slice(h * DV_GLA, (h + 1) * DV_GLA)
            qe_c = qe[rows, kk]
            v_c = v[rows, vv]
            a = jnp.where(tril, _dot(qe_c, kdT[kk, rows]), 0.0)
            s_old = s_ref[h]
            oacc_ref[rows, vv] = _dot(a.astype(BF16), v_c) + _dot(qe_c, s_old.astype(BF16))
            dec = decT[kk, c * C:c * C + 1]
            s_ref[h] = dec * s_old + _dot(keT[kk, rows], v_c)

    gr = _dot(xn, wr_ref[...])
    gate = gr * _sigmoid(gr)
    for h in range(H_GLA):
        vv = slice(h * DV_GLA, (h + 1) * DV_GLA)
        y = _rmsnorm(oacc_ref[:, vv], g_ref[...])
        o_ref[0, :, vv] = (y * gate[:, vv]).astype(o_ref.dtype)

    @pl.when(t == pl.num_programs(1) - 1)
    def _():
        st_ref[0] = s_ref[...]


def gla(xn, wq, wkT, wv, wa, waT, wr, wu, wuT, brow, bcol, g, T):
    B, S, D = xn.shape
    consts = (wq, wkT, wv, wa, waT, wr, wu, wuT, brow, bcol, g)
    return pl.pallas_call(
        _gla_kernel,
        grid=(B, S // T),
        in_specs=[pl.BlockSpec((1, T, D), lambda b, t: (b, t, 0))] + [_const_spec(w.shape) for w in consts],
        out_specs=(pl.BlockSpec((1, T, W_GLA_V), lambda b, t: (b, t, 0)),
                   pl.BlockSpec((1, H_GLA, DK_GLA, DV_GLA), lambda b, t: (b, 0, 0, 0))),
        out_shape=(jax.ShapeDtypeStruct((B, S, W_GLA_V), BF16),
                   jax.ShapeDtypeStruct((B, H_GLA, DK_GLA, DV_GLA), F32)),
        scratch_shapes=[pltpu.VMEM((H_GLA, DK_GLA, DV_GLA), F32), pltpu.VMEM((T, W_GLA_V), F32)],
        compiler_params=_params(("parallel", "arbitrary")),
        name="gla",
    )(xn, *consts)


def _mem_kv_kernel(m_ref, g_ref, wkT_ref, wvT_ref, kT_ref, vT_ref):
    mn = _rmsnorm(m_ref[0], g_ref[...]).astype(BF16)
    kT_ref[0] = _dot_nt(wkT_ref[...], mn)
    vT_ref[0] = _dot_nt(wvT_ref[...], mn)


def mem_kv(mem, g, wkT, wvT):
    B, M, D = mem.shape
    blk = pl.BlockSpec((1, W_MEM, M), lambda b: (b, 0, 0))
    return pl.pallas_call(
        _mem_kv_kernel,
        grid=(B,),
        in_specs=[pl.BlockSpec((1, M, D), lambda b: (b, 0, 0)), _const_spec(g.shape),
                  _const_spec(wkT.shape), _const_spec(wvT.shape)],
        out_specs=(blk, blk),
        out_shape=(jax.ShapeDtypeStruct((B, W_MEM, M), F32),) * 2,
        compiler_params=_params(("parallel",)),
        name="mem_kv",
    )(mem, g, wkT, wvT)


def _mem_attn_kernel(xn_ref, wq_ref, kT_ref, vT_ref, o_ref):
    q = _dot(xn_ref[0], wq_ref[...]).astype(BF16)
    outs = []
    for h in range(H_MEM):
        hh = slice(h * HD_MEM, (h + 1) * HD_MEM)
        s = _dot(q[:, hh], kT_ref[0, hh, :].astype(BF16)) * (HD_MEM ** -0.5)
        p = jnp.exp(s - jnp.max(s, axis=1, keepdims=True))
        p = p / jnp.sum(p, axis=1, keepdims=True)
        outs.append(_dot_nt(p.astype(BF16), vT_ref[0, hh, :].astype(BF16)))
    o_ref[0] = jnp.concatenate(outs, axis=1).astype(o_ref.dtype)


def mem_attn(xn, wq, kT, vT, tl):
    B, L, D = xn.shape
    M = kT.shape[2]
    return pl.pallas_call(
        _mem_attn_kernel,
        grid=(B, L // tl),
        in_specs=[pl.BlockSpec((1, tl, D), lambda b, i: (b, i, 0)), _const_spec(wq.shape),
                  pl.BlockSpec((1, W_MEM, M), lambda b, i: (b, 0, 0)),
                  pl.BlockSpec((1, W_MEM, M), lambda b, i: (b, 0, 0))],
        out_specs=pl.BlockSpec((1, tl, W_MEM), lambda b, i: (b, i, 0)),
        out_shape=jax.ShapeDtypeStruct((B, L, W_MEM), BF16),
        compiler_params=_params(("parallel", "parallel")),
        name="mem_attn",
    )(xn, wq, kT, vT)


HOST_CHUNKS = 4


def _run_host(n_own_in, n_own_out, chunk_fn, finish_fn, refs, plan):
    if plan is None:
        for c in range(HOST_CHUNKS):
            chunk_fn(c)
        finish_fn()
        return
    assert plan.groups == HOST_CHUNKS
    pt_ref, refs = refs[0], refs[1:]
    dec_in = refs[n_own_in:n_own_in + 7]
    dec_out = refs[n_own_in + 7 + n_own_out]
    dec_scratch = refs[n_own_in + 7 + n_own_out + 1:]
    _decode_side_task(plan, pl.program_id(0), pl.num_programs(0),
                      (pt_ref, *dec_in, dec_out, *dec_scratch), chunk_fn)
    finish_fn()


def _host_call(kernel, name, n_rows, tm, own_in_specs, own_out_spec, own_out_shape, own_args, decode):
    if decode is None:
        return pl.pallas_call(
            functools.partial(kernel, plan=None), grid=(n_rows // tm,), in_specs=own_in_specs,
            out_specs=own_out_spec, out_shape=own_out_shape, compiler_params=_params(("parallel",)),
            name=name)(*own_args), None
    plan, page_table, dec_args = decode
    dec_in_specs, dec_out_spec, dec_out_shape, dec_scratch = _decode_host_specs(plan)
    grid_spec = pltpu.PrefetchScalarGridSpec(
        num_scalar_prefetch=1, grid=(n_rows // tm,),
        in_specs=list(own_in_specs) + dec_in_specs,
        out_specs=(own_out_spec, dec_out_spec),
        scratch_shapes=dec_scratch)
    return pl.pallas_call(
        functools.partial(kernel, plan=plan), grid_spec=grid_spec,
        out_shape=(own_out_shape, dec_out_shape),
        compiler_params=_params(("arbitrary",)), name=name + "_decode")(page_table, *own_args, *dec_args)


def _merge_kernel(*refs, plan):
    own = refs[1:] if plan is not None else refs
    x_ref, xn_ref, fo_ref, go_ref, mo_ref, wgl_ref, wuf_ref, wug_ref, wum_ref, wout_ref = own[:10]
    h_ref = own[10 + (7 if plan is not None else 0)]
    D = x_ref.shape[1]
    branches = ((fo_ref, wuf_ref), (go_ref, wug_ref), (mo_ref, wum_ref))
    merged = []

    def chunk(c):
        if c < N_BRANCH:
            br, w = branches[c]
            gate = _sigmoid(_dot(xn_ref[...], wgl_ref[:, c * D:(c + 1) * D]))
            merged.append(gate * _dot(br[...], w[...]))
        else:
            m = (merged[0] + merged[1] + merged[2]).astype(BF16)
            h_ref[...] = x_ref[...] + _dot(m, wout_ref[...])

    _run_host(10, 1, chunk, lambda: None, refs, plan)


def merge(x, xn, fo, go, mo, wgl, wuf, wug, wum, wout, tm, decode=None):
    N, D = x.shape
    row = lambda n: pl.BlockSpec((tm, n), lambda i, *_: (i, 0))
    consts = (wgl, wuf, wug, wum, wout)
    in_specs = [row(D), row(D), row(fo.shape[1]), row(go.shape[1]), row(mo.shape[1])] \
        + [_const_spec(w.shape) for w in consts]
    return _host_call(_merge_kernel, "merge", N, tm, in_specs, row(D), jax.ShapeDtypeStruct((N, D), F32),
                      (x, xn, fo, go, mo, *consts), decode)


def _mlp_kernel(*refs, plan):
    own = refs[1:] if plan is not None else refs
    h_ref, g_ref, w1_ref, w2_ref, gf_ref = own[:5]
    y_ref = own[5 + (7 if plan is not None else 0)]
    fc = w1_ref.shape[1] // HOST_CHUNKS
    state = {}

    def chunk(c):
        if c == 0:
            state["acc"] = h_ref[...]
            state["hn"] = _rmsnorm(state["acc"], g_ref[...]).astype(BF16)
        cc = slice(c * fc, (c + 1) * fc)
        u = jnp.maximum(_dot(state["hn"], w1_ref[:, cc]), 0.0)
        state["acc"] = state["acc"] + _dot((u * u).astype(BF16), w2_ref[cc, :])

    def finish():
        y_ref[...] = _rmsnorm(state["acc"], gf_ref[...])

    _run_host(5, 1, chunk, finish, refs, plan)


def mlp(h, g, w1, w2, gf, tm, decode=None):
    N, D = h.shape
    row = pl.BlockSpec((tm, D), lambda i, *_: (i, 0))
    in_specs = [row, _const_spec(g.shape), _const_spec(w1.shape), _const_spec(w2.shape), _const_spec(gf.shape)]
    return _host_call(_mlp_kernel, "mlp", N, tm, in_specs, row, jax.ShapeDtypeStruct((N, D), F32),
                      (h, g, w1, w2, gf), decode)


def _sample_proj_kernel(x_ref, g_ref, w_ref, xn_ref, o_ref):
    xn = _rmsnorm(x_ref[...], g_ref[...]).astype(BF16)
    xn_ref[...] = xn
    o_ref[...] = _dot(xn, w_ref[...])


def sample_proj(x, g, w, tn):
    N, D = x.shape
    NP = w.shape[1]
    return pl.pallas_call(
        _sample_proj_kernel,
        grid=(NP // tn,),
        in_specs=[_const_spec((N, D)), _const_spec(g.shape), pl.BlockSpec((D, tn), lambda j: (0, j))],
        out_specs=(_const_spec((N, D)), pl.BlockSpec((N, tn), lambda j: (0, j))),
        out_shape=(jax.ShapeDtypeStruct((N, D), BF16), jax.ShapeDtypeStruct((N, NP), F32)),
        compiler_params=_params(("arbitrary",)),
        name="sample_proj",
    )(x, g, w)


def _logsig_kernel(ff_ref, b_ref, o_ref):
    o_ref[...] = _log_sigmoid(ff_ref[...] + b_ref[...])


def logsig_bias(ff, b):
    return pl.pallas_call(_logsig_kernel, out_shape=jax.ShapeDtypeStruct(ff.shape, F32), name="logsig")(ff, b)


class _DecodePlan(NamedTuple):
    first_b: int
    n_b: int
    steps_per_b: int
    groups: int
    pages: int
    n_pages: int


def _plan_decode(first_b, n_b, n_steps, n_pages, pages):
    steps_per_b, rem = divmod(n_steps, n_b)
    assert rem == 0 and steps_per_b >= 1, (n_steps, n_b)
    per_step, rem = divmod(n_pages, steps_per_b)
    assert rem == 0, (n_pages, steps_per_b)
    groups, rem = divmod(per_step, pages)
    assert rem == 0 and groups % 2 == 0, (per_step, pages)
    return _DecodePlan(first_b, n_b, steps_per_b, groups, pages, n_pages)


def _decode_side_task(plan, i, n_steps, refs, host_chunk):
    (pt_ref, q_ref, kn_ref, vn_ref, lfn_ref, k_hbm, v_hbm, lf_hbm, o_ref,
     kbuf, vbuf, lfbuf, sem, qs_ref, m_ref, l_ref, r_ref, acc_ref) = refs
    G = plan.pages
    part = i % plan.steps_per_b
    lane = lax.broadcasted_iota(jnp.int32, (H_FOX, LANES), 1)
    lane_w = lax.broadcasted_iota(jnp.int32, (W_FOX, LANES), 1)

    def group_copies(step, g, slot):
        b = plan.first_b + step // plan.steps_per_b
        done = (step % plan.steps_per_b) * plan.groups * G + g * G
        copies = []
        for k in range(G):
            page = pt_ref[b, plan.n_pages - 1 - (done + k)]
            copies += [pltpu.make_async_copy(k_hbm.at[page], kbuf.at[slot, k], sem.at[slot, 0]),
                       pltpu.make_async_copy(v_hbm.at[page], vbuf.at[slot, k], sem.at[slot, 1]),
                       pltpu.make_async_copy(lf_hbm.at[page], lfbuf.at[slot, k], sem.at[slot, 2])]
        return copies

    def head_scores(k_ref):
        return jnp.concatenate(
            [jnp.sum(k_ref[h * HD_FOX:(h + 1) * HD_FOX, :] * qs_ref[h * HD_FOX:(h + 1) * HD_FOX, :],
                     axis=0, keepdims=True) for h in range(H_FOX)], axis=0)

    @pl.when(i == 0)
    def _():
        for c in group_copies(0, 0, 0):
            c.start()

    @pl.when(part == 0)
    def _():
        qs_ref[...] = q_ref[0] * (HD_FOX ** -0.5)
        m_ref[...] = head_scores(kn_ref.at[0])
        l_ref[...] = jnp.where(lane == 0, 1.0, 0.0)
        r_ref[...] = jnp.zeros_like(r_ref)
        acc_ref[...] = jnp.where(lane_w == 0, vn_ref[0], 0.0)

    for g in range(plan.groups):
        slot = g % 2
        if g + 1 < plan.groups:
            for c in group_copies(i, g + 1, 1 - slot):
                c.start()
        else:
            @pl.when(i + 1 < n_steps)
            def _():
                for c in group_copies(i + 1, 0, 1 - slot):
                    c.start()
        host_chunk(g)
        for c in group_copies(i, g, slot):
            c.wait()

        r_run = r_ref[...]
        scores = []
        for k in range(G):
            lf = lfbuf[slot, k]
            incl = _lane_cumsum(lf, reverse=True)
            scores.append(head_scores(kbuf.at[slot, k]) + lfn_ref[0] + (incl - lf + r_run))
            r_run = r_run + incl[:, 0:1]
        r_ref[...] = r_run
        m_prev = m_ref[...]
        m_new = m_prev
        for s in scores:
            m_new = jnp.maximum(m_new, jnp.max(s, axis=1, keepdims=True))
        alpha = jnp.exp(m_prev - m_new)
        probs = [jnp.exp(s - m_new) for s in scores]
        l_ref[...] = alpha * l_ref[...] + functools.reduce(lambda a, b: a + b, probs)
        m_ref[...] = m_new
        for h in range(H_FOX):
            rows = slice(h * HD_FOX, (h + 1) * HD_FOX)
            a = acc_ref[rows, :] * alpha[h:h + 1, :]
            for k in range(G):
                a = a + vbuf[slot, k, rows, :] * probs[k][h:h + 1, :]
            acc_ref[rows, :] = a

    @pl.when(part == plan.steps_per_b - 1)
    def _():
        inv = 1.0 / jnp.sum(l_ref[...], axis=1, keepdims=True)
        for h in range(H_FOX):
            rows = slice(h * HD_FOX, (h + 1) * HD_FOX)
            acc_ref[rows, :] = acc_ref[rows, :] * inv[h:h + 1, :]
        ones = jnp.ones((8, LANES), BF16)
        out = None
        for piece in _split_bf16(acc_ref[...], 3):
            t = _dot_nt(ones, piece)
            out = t if out is None else out + t
        o_ref[0] = out[0:1, :]


def _decode_host_specs(plan):
    b_of = lambda i: plan.first_b + i // plan.steps_per_b
    per_b = lambda n: pl.BlockSpec((1, n, LANES), lambda i, pt: (b_of(i), 0, 0))
    pool = pl.BlockSpec(memory_space=pl.ANY)
    in_specs = [per_b(W_FOX), per_b(W_FOX), per_b(W_FOX), per_b(H_FOX), pool, pool, pool]
    out_spec = pl.BlockSpec((1, 1, W_FOX), lambda i, pt: (i // plan.steps_per_b, 0, 0))
    out_shape = jax.ShapeDtypeStruct((plan.n_b, 1, W_FOX), F32)
    G = plan.pages
    scratch = [pltpu.VMEM((2, G, W_FOX, LANES), F32), pltpu.VMEM((2, G, W_FOX, LANES), F32),
               pltpu.VMEM((2, G, H_FOX, LANES), F32), pltpu.SemaphoreType.DMA((2, 3)),
               pltpu.VMEM((W_FOX, LANES), F32), pltpu.VMEM((H_FOX, LANES), F32),
               pltpu.VMEM((H_FOX, LANES), F32), pltpu.VMEM((H_FOX, LANES), F32),
               pltpu.VMEM((W_FOX, LANES), F32)]
    return in_specs, out_spec, out_shape, scratch


def _sample_gla_kernel(q_ref, k_ref, ga_ref, v_ref, gr_ref, s0_ref, wuT_ref, bcol_ref, g_ref, o_ref, st_ref):
    z = _dot(wuT_ref[...], ga_ref[0].astype(BF16)) + bcol_ref[...]
    la = _log_sigmoid(z) / GLA_NORMALIZER
    ea = jnp.exp(la)
    k = k_ref[0]
    qe = q_ref[0] * (DK_GLA ** -0.5) * ea
    kd = k * jnp.exp(-la)
    gr = gr_ref[0]
    gate = gr * _sigmoid(gr)
    wide = lambda a: jnp.concatenate([a, a], axis=1)
    for h in range(H_GLA):
        kk = slice(h * DK_GLA, (h + 1) * DK_GLA)
        vv = slice(h * DV_GLA, (h + 1) * DV_GLA)
        s_old = s0_ref[0, h]
        v_h = v_ref[0][:, vv]
        a = jnp.sum(qe[kk] * kd[kk], axis=0, keepdims=True)
        o = wide(a) * v_h + jnp.sum(wide(qe[kk]) * s_old, axis=0, keepdims=True)
        st_ref[0, h] = wide(ea[kk]) * s_old + wide(k[kk]) * v_h
        o_ref[0, :, vv] = _rmsnorm(o, g_ref[...]) * gate[:, vv]


def sample_gla(q_rep, k_rep, ga_rep, v, gr, s0, wuT, bcol, g):
    DB = q_rep.shape[0]
    per_b = lambda a: pl.BlockSpec((1,) + a.shape[1:], lambda b: (b,) + (0,) * (a.ndim - 1))
    return pl.pallas_call(
        _sample_gla_kernel,
        grid=(DB,),
        in_specs=[per_b(q_rep), per_b(k_rep), per_b(ga_rep), per_b(v), per_b(gr), per_b(s0),
                  _const_spec(wuT.shape), _const_spec(bcol.shape), _const_spec(g.shape)],
        out_specs=(per_b(v), per_b(s0)),
        out_shape=(jax.ShapeDtypeStruct(v.shape, F32), jax.ShapeDtypeStruct(s0.shape, F32)),
        compiler_params=_params(("parallel",)),
        name="sample_gla",
    )(q_rep, k_rep, ga_rep, v, gr, s0, wuT, bcol, g)


_SPLITS = (W_FOX, W_FOX, W_FOX, H_FOX, W_GLA_K, W_GLA_K, W_GLA_V, GLA_RANK, W_GLA_V, W_MEM)
PROMPT_TILE = 512
GLA_TILE = 256
ROW_TILE = 256
DECODE_PAGES = 4
SAMPLE_PROJ_TILE = 512
SAMPLE_MEM_ROWS = 8


def _lane_rep(a):
    return jnp.broadcast_to(a[..., None], a.shape + (LANES,))


def kernel(x_prompt, x_sample, mem_prompt, cache_fox_k, cache_fox_v, cache_fox_logf, state_gla, cache_mem_k, cache_mem_v, page_table, g_attn, w_in, b_fox_f, w_gla_gate_up, b_gla_gate, g_gla_norm, g_mem, w_mem_k, w_mem_v, w_up_fox, w_up_gla, w_up_mem, w_out, g_mlp, w_mlp_in, w_mlp_out, g_final):
    B, S, D = x_prompt.shape
    DB = x_sample.shape[0]
    assert w_in.shape[0] == 1 and x_sample.shape[1] == 1

    w = w_in[0]
    segs, off = [], 0
    for n in _SPLITS:
        segs.append(w[:, off:off + n])
        off += n
    w_fq, w_fk, w_fv, w_ff, w_gq, w_gk, w_gv, w_ga, w_gr, w_mq = segs
    w_gl = w[:, off:]
    bf = lambda a: a.astype(BF16)
    bfT = lambda a: a.T.astype(BF16)
    row = lambda a: a.reshape(1, -1)
    col = lambda a: a.reshape(-1, 1)

    ga_args = (bf(w_gq), bfT(w_gk), bf(w_gv), bf(w_ga), bfT(w_ga), bf(w_gr), bf(w_gla_gate_up[0]),
               bfT(w_gla_gate_up[0]), row(b_gla_gate[0]), col(b_gla_gate[0]), row(g_gla_norm[0]))
    merge_w = (bf(w_gl), bf(w_up_fox[0]), bf(w_up_gla[0]), bf(w_up_mem[0]), bf(w_out[0]))
    mlp_w = (row(g_mlp[0]), bf(w_mlp_in[0]), bf(w_mlp_out[0]), row(g_final))
    g_attn_row = row(g_attn[0])

    xs = x_sample.reshape(DB, D)
    d_in = w.shape[1]
    d_pad = -(-d_in // SAMPLE_PROJ_TILE) * SAMPLE_PROJ_TILE
    xn_s, proj = sample_proj(xs, g_attn_row, bf(jnp.pad(w, ((0, 0), (0, d_pad - d_in)))), SAMPLE_PROJ_TILE)
    parts, off = [], 0
    for n in _SPLITS[:-1]:
        parts.append(proj[:, off:off + n])
        off += n
    s_fq, s_fk, s_fv, s_ff, s_gq, s_gk, s_gv, s_ga, s_gr = parts
    lf_new = logsig_bias(s_ff, row(b_fox_f[0]))

    pool_T = lambda c: c.transpose(0, 2, 3, 1).reshape(c.shape[0], W_FOX, c.shape[1])
    decode_args = (_lane_rep(s_fq), _lane_rep(s_fk), _lane_rep(s_fv), _lane_rep(lf_new),
                   pool_T(cache_fox_k[0]), pool_T(cache_fox_v[0]), cache_fox_logf[0].transpose(0, 2, 1))

    xn, qT, ka, kT, vT, vT16, lfT, cT = fox_proj(
        x_prompt, g_attn_row, bfT(w_fq), bf(w_fk), bfT(w_fv), bfT(w_ff), col(b_fox_f[0]), PROMPT_TILE)
    fox_o = fox_attn(qT, cT, ka, vT16, PROMPT_TILE)
    gla_o, p_state = gla(xn, *ga_args, GLA_TILE)
    mkT, mvT = mem_kv(mem_prompt, row(g_mem[0]), bfT(w_mem_k[0]), bfT(w_mem_v[0]))
    mem_o = mem_attn(xn, bf(w_mq), mkT, mvT, PROMPT_TILE)
    N = B * S
    n_steps = N // ROW_TILE
    half = DB // 2
    n_pages = page_table.shape[1]
    plan_a = _plan_decode(0, half, n_steps, n_pages, DECODE_PAGES)
    plan_b = _plan_decode(half, DB - half, n_steps, n_pages, DECODE_PAGES)
    h, fox_o_a = merge(x_prompt.reshape(N, D), xn.reshape(N, D), fox_o.reshape(N, W_FOX),
                       gla_o.reshape(N, W_GLA_V), mem_o.reshape(N, W_MEM), *merge_w, ROW_TILE,
                       decode=(plan_a, page_table, decode_args))
    y_prompt, fox_o_b = mlp(h, *mlp_w, ROW_TILE, decode=(plan_b, page_table, decode_args))
    y_prompt = y_prompt.reshape(B, S, D)
    fox_o_s = jnp.concatenate([fox_o_a, fox_o_b], axis=0)

    heads_out = lambda aT, nh, hd: aT.reshape(aT.shape[0], nh, hd, aT.shape[2]).transpose(0, 3, 1, 2)[None]
    p_fox_k = heads_out(kT, H_FOX, HD_FOX)
    p_fox_v = heads_out(vT, H_FOX, HD_FOX)
    p_fox_logf = lfT.transpose(0, 2, 1)[None]
    p_mem_k = heads_out(mkT, H_MEM, HD_MEM)
    p_mem_v = heads_out(mvT, H_MEM, HD_MEM)

    gla_o_s, s_state = sample_gla(_lane_rep(s_gq), _lane_rep(s_gk), _lane_rep(s_ga), s_gv[:, None, :],
                                  s_gr[:, None, :], state_gla[0], bfT(w_gla_gate_up[0]), col(b_gla_gate[0]),
                                  row(g_gla_norm[0]))
    mem_T = lambda c: c.transpose(0, 2, 3, 1).reshape(c.shape[0], W_MEM, c.shape[1])
    xn_rows = jnp.broadcast_to(xn_s[:, None, :], (DB, SAMPLE_MEM_ROWS, D))
    mem_o_s = mem_attn(xn_rows, bf(w_mq), mem_T(cache_mem_k[0]), mem_T(cache_mem_v[0]), SAMPLE_MEM_ROWS)[:, 0]
    h_s, _ = merge(xs, xn_s, bf(fox_o_s.reshape(DB, W_FOX)), bf(gla_o_s.reshape(DB, W_GLA_V)), mem_o_s,
                   *merge_w, DB)
    y_sample, _ = mlp(h_s, *mlp_w, DB)
    y_sample = y_sample.reshape(DB, 1, D)

    s_fox_k = s_fk.reshape(1, DB, 1, H_FOX, HD_FOX)
    s_fox_v = s_fv.reshape(1, DB, 1, H_FOX, HD_FOX)
    s_fox_logf = lf_new.reshape(1, DB, 1, H_FOX)
    return (y_prompt, y_sample, p_fox_k, p_fox_v, p_fox_logf, p_state[None], p_mem_k, p_mem_v,
            s_fox_k, s_fox_v, s_fox_logf, s_state[None])
```

```python
import functools
from typing import NamedTuple

import jax
import jax.numpy as jnp
from jax import lax
from jax.experimental import pallas as pl
from jax.experimental.pallas import tpu as pltpu

F32 = jnp.float32
BF16 = jnp.bfloat16

H_FOX = 16
HD_FOX = 64
W_FOX = H_FOX * HD_FOX
H_GLA = 4
DK_GLA = 128
DV_GLA = 256
W_GLA_K = H_GLA * DK_GLA
W_GLA_V = H_GLA * DV_GLA
GLA_RANK = 16
GLA_NORMALIZER = 16.0
GLA_CHUNK = 64
MEM_LEN = 256
H_MEM = 4
HD_MEM = 64
W_MEM = H_MEM * HD_MEM
N_BRANCH = 3
EPS = 1e-6

LANES = 128
FOX_AUG = 128
VMEM_LIMIT = 56 * 1024 * 1024


def _dot(a, b):
    return jnp.dot(a, b, preferred_element_type=F32)


def _dot_nt(a, b):
    return lax.dot_general(a, b, (((1,), (1,)), ((), ())), preferred_element_type=F32)


def _split_bf16(x, parts):
    out = []
    r = x
    for _ in range(parts):
        p = r.astype(BF16)
        out.append(p)
        r = r - p.astype(F32)
    return out


def _log_sigmoid(x):
    return jnp.minimum(x, 0.0) - jnp.log1p(jnp.exp(-jnp.abs(x)))


def _sigmoid(x):
    return 1.0 / (1.0 + jnp.exp(-x))


def _rmsnorm(x, g):
    ms = jnp.mean(x * x, axis=-1, keepdims=True)
    return x * lax.rsqrt(ms + EPS) * g


def _lane_cumsum(x, reverse=False):
    rows, n = x.shape
    lane = lax.broadcasted_iota(jnp.int32, (rows, LANES), 1)
    blocks = []
    for i in range(n // LANES):
        y = x[:, i * LANES:(i + 1) * LANES]
        k = 1
        while k < LANES:
            if reverse:
                y = y + jnp.where(lane < LANES - k, pltpu.roll(y, LANES - k, 1), 0.0)
            else:
                y = y + jnp.where(lane >= k, pltpu.roll(y, k, 1), 0.0)
            k *= 2
        blocks.append(y)
    order = range(len(blocks) - 1, -1, -1) if reverse else range(len(blocks))
    edge = 0 if reverse else LANES - 1
    carry = None
    for i in order:
        if carry is not None:
            blocks[i] = blocks[i] + carry
        carry = blocks[i][:, edge:edge + 1]
    return blocks[0] if len(blocks) == 1 else jnp.concatenate(blocks, axis=1)


def _const_spec(shape):
    return pl.BlockSpec(shape, lambda *_: (0,) * len(shape))


def _params(sem):
    return pltpu.CompilerParams(dimension_semantics=sem, vmem_limit_bytes=VMEM_LIMIT)


def _split3_f32(x):
    hi = x.astype(BF16).astype(F32)
    r = x - hi
    mid = r.astype(BF16).astype(F32)
    lo = (r - mid).astype(BF16).astype(F32)
    return hi, mid, lo


def _fox_proj_kernel(x_ref, g_ref, wqT_ref, wk_ref, wvT_ref, wffT_ref, bcol_ref, e_ref,
                     xn_ref, qT_ref, ka_ref, kT_ref, vT_ref, vT16_ref, lfT_ref, cT_ref, carry_ref):
    i = pl.program_id(1)
    ts = x_ref.shape[1]

    @pl.when(i == 0)
    def _():
        carry_ref[...] = jnp.zeros_like(carry_ref)

    xn = _rmsnorm(x_ref[0], g_ref[...]).astype(BF16)
    xn_ref[0] = xn
    qT_ref[0] = (_dot_nt(wqT_ref[...], xn) * (HD_FOX ** -0.5)).astype(BF16)
    vT = _dot_nt(wvT_ref[...], xn)
    vT_ref[0] = vT
    vT16_ref[0] = vT.astype(BF16)
    k = _dot(xn, wk_ref[...])
    kT_ref[0] = k.T

    lfT = _log_sigmoid(_dot_nt(wffT_ref[...], xn) + bcol_ref[...])
    lfT_ref[0] = lfT
    cT = _lane_cumsum(lfT) + carry_ref[...]
    carry_ref[...] = cT[:, ts - 1:ts]
    cT_ref[0] = cT

    hi, mid, lo = _split3_f32(-cT)
    stack = jnp.concatenate([hi, mid, lo, jnp.ones((H_FOX, ts), F32),
                             jnp.zeros((LANES - 4 * H_FOX, ts), F32)], axis=0)
    ex = _dot(stack.T.astype(BF16), e_ref[...])
    lower = lax.broadcasted_iota(jnp.int32, (ts, LANES), 1) < HD_FOX
    for g in range(H_FOX // 2):
        kg = k[:, g * LANES:(g + 1) * LANES]
        eg = ex[:, g * LANES:(g + 1) * LANES]
        ka_ref[0, :, (2 * g) * FOX_AUG:(2 * g + 1) * FOX_AUG] = jnp.where(lower, kg, eg).astype(BF16)
        ka_ref[0, :, (2 * g + 1) * FOX_AUG:(2 * g + 2) * FOX_AUG] = jnp.where(lower, eg, kg).astype(BF16)


def _fox_bias_selector():
    e = jnp.zeros((LANES, W_FOX), F32)
    for h in range(H_FOX):
        base = (h // 2) * LANES + (HD_FOX if h % 2 == 0 else 0)
        for j in range(3):
            e = e.at[3 * H_FOX, base + j].set(1.0)
            e = e.at[j * H_FOX + h, base + 3 + j].set(1.0)
    return e.astype(BF16)


def fox_proj(x, g, wqT, wk, wvT, wffT, bcol, ts):
    B, S, D = x.shape
    e = _fox_bias_selector()
    out_shape = (
        jax.ShapeDtypeStruct((B, S, D), BF16),
        jax.ShapeDtypeStruct((B, W_FOX, S), BF16),
        jax.ShapeDtypeStruct((B, S, H_FOX * FOX_AUG), BF16),
        jax.ShapeDtypeStruct((B, W_FOX, S), F32),
        jax.ShapeDtypeStruct((B, W_FOX, S), F32),
        jax.ShapeDtypeStruct((B, W_FOX, S), BF16),
        jax.ShapeDtypeStruct((B, H_FOX, S), F32),
        jax.ShapeDtypeStruct((B, H_FOX, S), F32),
    )
    tile = lambda n: pl.BlockSpec((1, ts, n), lambda b, i: (b, i, 0))
    tileT = lambda n: pl.BlockSpec((1, n, ts), lambda b, i: (b, 0, i))
    consts = (g, wqT, wk, wvT, wffT, bcol, e)
    return pl.pallas_call(
        _fox_proj_kernel,
        grid=(B, S // ts),
        in_specs=[tile(D)] + [_const_spec(c.shape) for c in consts],
        out_specs=(tile(D), tileT(W_FOX), tile(H_FOX * FOX_AUG), tileT(W_FOX), tileT(W_FOX),
                   tileT(W_FOX), tileT(H_FOX), tileT(H_FOX)),
        out_shape=out_shape,
        scratch_shapes=[pltpu.VMEM((H_FOX, 1), F32)],
        compiler_params=_params(("parallel", "arbitrary")),
        name="fox_proj",
    )(x, *consts)


L_ROWS = 16
FOX_HEAD_GROUP = 4


def _fox_attn_kernel(qi_ref, kj_ref, qT_ref, cT_ref, ka_ref, vT_ref, o_ref,
                     qa_ref, s_ref, mx_ref, m_ref, acc_ref, *, tq):
    group = pl.program_id(1)
    n_heads = qa_ref.shape[0]
    S = qT_ref.shape[2]
    tk = tq
    n_pairs = qi_ref.shape[0]

    row8 = lax.broadcasted_iota(jnp.int32, (8, S), 0)
    for h in range(n_heads):
        hi, mid, lo = _split3_f32(cT_ref[0, pl.ds(n_heads * group + h, 1), :])
        ext8 = jnp.where(row8 == 0, hi, jnp.where(row8 == 1, mid, jnp.where(
            row8 == 2, lo, jnp.where(row8 < 6, 1.0, 0.0))))
        ext = jnp.concatenate([ext8, jnp.zeros((HD_FOX - 8, S), F32)], axis=0).astype(BF16)
        qh = qT_ref[0, h * HD_FOX:(h + 1) * HD_FOX, :]
        qa_ref[h] = jnp.concatenate([qh, ext] if h % 2 == 0 else [ext, qh], axis=0)
        m_ref[h] = jnp.full((1, tq), -jnp.inf, F32)
        acc_ref[h] = jnp.zeros((HD_FOX + L_ROWS, tq), F32)

    krow = lax.broadcasted_iota(jnp.int32, (tk, tq), 0)
    qcol = lax.broadcasted_iota(jnp.int32, (tk, tq), 1)
    causal = krow <= qcol
    ones = jnp.ones((L_ROWS, tk), BF16)

    def scores(t, slot, h):
        qoff = pl.multiple_of(qi_ref[t] * tq, tq)
        koff = pl.multiple_of(kj_ref[t] * tk, tk)
        sT = _dot(ka_ref[0, pl.ds(koff, tk), h * FOX_AUG:(h + 1) * FOX_AUG],
                  qa_ref[h, :, pl.ds(qoff, tq)])
        s_ref[slot, h] = sT
        mx_ref[slot, h] = jnp.max(sT, axis=0, keepdims=True)

    def consume(t, slot, h, diagonal):
        koff = pl.multiple_of(kj_ref[t] * tk, tk)
        sT = s_ref[slot, h]
        if diagonal:
            sT = jnp.where(causal, sT, -jnp.inf)
            mx = jnp.max(sT, axis=0, keepdims=True)
        else:
            mx = mx_ref[slot, h]
        m_prev = m_ref[h]
        m_next = jnp.maximum(m_prev, mx)
        pT = jnp.exp(sT - m_next).astype(BF16)
        alpha = jnp.exp(m_prev - m_next)
        va = jnp.concatenate([vT_ref[0, h * HD_FOX:(h + 1) * HD_FOX, pl.ds(koff, tk)], ones], axis=0)
        acc = alpha * acc_ref[h] + _dot(va, pT)
        if not diagonal:
            m_ref[h] = m_next
            acc_ref[h] = acc
            return None
        m_ref[h] = jnp.full((1, tq), -jnp.inf, F32)
        acc_ref[h] = jnp.zeros((HD_FOX + L_ROWS, tq), F32)
        return acc[:HD_FOX] / acc[HD_FOX:HD_FOX + 1]

    def trip(t, slot, diagonal):
        nxt = jnp.minimum(t + 1, n_pairs - 1)
        outs = []
        for h in range(n_heads):
            scores(nxt, 1 - slot, h)
            outs.append(consume(t, slot, h, diagonal))
        if diagonal:
            qoff = pl.multiple_of(qi_ref[t] * tq, tq)
            o_ref[0, pl.ds(qoff, tq), :] = jnp.concatenate(outs, axis=0).T.astype(o_ref.dtype)

    def either(t, slot):
        is_diag = kj_ref[t] == qi_ref[t]

        @pl.when(is_diag)
        def _():
            trip(t, slot, True)

        @pl.when(jnp.logical_not(is_diag))
        def _():
            trip(t, slot, False)

    for h in range(n_heads):
        scores(0, 0, h)

    def body(u, carry):
        either(2 * u, 0)
        either(2 * u + 1, 1)
        return carry

    lax.fori_loop(0, n_pairs // 2, body, 0)
    if n_pairs % 2:
        either(n_pairs - 1, 0)


def fox_attn(qT, cT, ka, vT16, tq):
    B, _, S = qT.shape
    nq = S // tq
    qi = jnp.asarray([i for i in range(nq) for _ in range(i + 1)], jnp.int32)
    kj = jnp.asarray([j for i in range(nq) for j in range(i + 1)], jnp.int32)
    hg = FOX_HEAD_GROUP
    grid_spec = pltpu.PrefetchScalarGridSpec(
        num_scalar_prefetch=2,
        grid=(B, H_FOX // hg),
        in_specs=[pl.BlockSpec((1, hg * HD_FOX, S), lambda b, p, *_: (b, p, 0)),
                  pl.BlockSpec((1, H_FOX, S), lambda b, p, *_: (b, 0, 0)),
                  pl.BlockSpec((1, S, hg * FOX_AUG), lambda b, p, *_: (b, 0, p)),
                  pl.BlockSpec((1, hg * HD_FOX, S), lambda b, p, *_: (b, p, 0))],
        out_specs=pl.BlockSpec((1, S, hg * HD_FOX), lambda b, p, *_: (b, 0, p)),
        scratch_shapes=[pltpu.VMEM((hg, FOX_AUG, S), BF16),
                        pltpu.VMEM((2, hg, tq, tq), F32),
                        pltpu.VMEM((2, hg, 1, tq), F32),
                        pltpu.VMEM((hg, 1, tq), F32),
                        pltpu.VMEM((hg, HD_FOX + L_ROWS, tq), F32)],
    )
    return pl.pallas_call(
        functools.partial(_fox_attn_kernel, tq=tq),
        grid_spec=grid_spec,
        out_shape=jax.ShapeDtypeStruct((B, S, W_FOX), BF16),
        compiler_params=_params(("parallel", "parallel")),
        name="fox_attn",
    )(qi, kj, qT, cT, ka, vT16)


def _gla_kernel(xn_ref, wqT_ref, wkT_ref, wv_ref, waT_ref, wr_ref, wuT_ref, bcol_ref, g_ref,
                o_ref, st_ref, s_ref, oacc_ref):
    t = pl.program_id(1)
    T = xn_ref.shape[1]
    C = GLA_CHUNK

    @pl.when(t == 0)
    def _():
        s_ref[...] = jnp.zeros_like(s_ref)

    xn = xn_ref[0]
    qT = _dot_nt(wqT_ref[...], xn) * (DK_GLA ** -0.5)
    kT = _dot_nt(wkT_ref[...], xn)
    v = _dot(xn, wv_ref[...]).astype(BF16)
    gaT = _dot_nt(waT_ref[...], xn).astype(BF16)
    laT = _log_sigmoid(_dot(wuT_ref[...], gaT) + bcol_ref[...]) / GLA_NORMALIZER

    r = lax.broadcasted_iota(jnp.int32, (T, T), 0)
    c_ = lax.broadcasted_iota(jnp.int32, (T, T), 1)
    same = (r // C) == (c_ // C)
    upper = jnp.logical_and(same, r <= c_).astype(BF16)
    block = same.astype(BF16)
    bT = totT = None
    for part in _split_bf16(laT, 2):
        d = _dot(part, upper)
        bT = d if bT is None else bT + d
        d = _dot(part, block)
        totT = d if totT is None else totT + d

    qe = (qT * jnp.exp(bT)).T.astype(BF16)
    kdT = (kT * jnp.exp(-bT)).astype(BF16)
    keT = (kT * jnp.exp(totT - bT)).astype(BF16)
    decT = jnp.exp(totT)
    in_chunk_causal = jnp.logical_and(same, c_ <= r)

    n_chunks = T // C
    for h in range(H_GLA):
        kk = slice(h * DK_GLA, (h + 1) * DK_GLA)
        vv = slice(h * DV_GLA, (h + 1) * DV_GLA)
        a = jnp.where(in_chunk_causal, _dot(qe[:, kk], kdT[kk, :]), 0.0)
        o_intra = _dot(a.astype(BF16), v[:, vv])
        ds = [_dot(keT[kk, c * C:(c + 1) * C], v[c * C:(c + 1) * C, vv]) for c in range(n_chunks)]
        s = s_ref[h]
        for c in range(n_chunks):
            rows = slice(c * C, (c + 1) * C)
            oacc_ref[rows, vv] = o_intra[rows] + _dot(qe[rows, kk], s.astype(BF16))
            s = decT[kk, c * C:c * C + 1] * s + ds[c]
        s_ref[h] = s

    gr = _dot(xn, wr_ref[...])
    gate = gr * _sigmoid(gr)
    for h in range(H_GLA):
        vv = slice(h * DV_GLA, (h + 1) * DV_GLA)
        y = _rmsnorm(oacc_ref[:, vv], g_ref[...])
        o_ref[0, :, vv] = (y * gate[:, vv]).astype(o_ref.dtype)

    @pl.when(t == pl.num_programs(1) - 1)
    def _():
        st_ref[0] = s_ref[...]


def gla(xn, wqT, wkT, wv, waT, wr, wuT, bcol, g, T):
    B, S, D = xn.shape
    consts = (wqT, wkT, wv, waT, wr, wuT, bcol, g)
    return pl.pallas_call(
        _gla_kernel,
        grid=(B, S // T),
        in_specs=[pl.BlockSpec((1, T, D), lambda b, t: (b, t, 0))] + [_const_spec(w.shape) for w in consts],
        out_specs=(pl.BlockSpec((1, T, W_GLA_V), lambda b, t: (b, t, 0)),
                   pl.BlockSpec((1, H_GLA, DK_GLA, DV_GLA), lambda b, t: (b, 0, 0, 0))),
        out_shape=(jax.ShapeDtypeStruct((B, S, W_GLA_V), BF16),
                   jax.ShapeDtypeStruct((B, H_GLA, DK_GLA, DV_GLA), F32)),
        scratch_shapes=[pltpu.VMEM((H_GLA, DK_GLA, DV_GLA), F32), pltpu.VMEM((T, W_GLA_V), F32)],
        compiler_params=_params(("parallel", "arbitrary")),
        name="gla",
    )(xn, *consts)


def _mem_kv_kernel(m_ref, g_ref, wkT_ref, wvT_ref, kT_ref, vT_ref):
    mn = _rmsnorm(m_ref[0], g_ref[...]).astype(BF16)
    kT_ref[0] = _dot_nt(wkT_ref[...], mn)
    vT_ref[0] = _dot_nt(wvT_ref[...], mn)


def mem_kv(mem, g, wkT, wvT):
    B, M, D = mem.shape
    blk = pl.BlockSpec((1, W_MEM, M), lambda b: (b, 0, 0))
    return pl.pallas_call(
        _mem_kv_kernel,
        grid=(B,),
        in_specs=[pl.BlockSpec((1, M, D), lambda b: (b, 0, 0)), _const_spec(g.shape),
                  _const_spec(wkT.shape), _const_spec(wvT.shape)],
        out_specs=(blk, blk),
        out_shape=(jax.ShapeDtypeStruct((B, W_MEM, M), F32),) * 2,
        compiler_params=_params(("parallel",)),
        name="mem_kv",
    )(mem, g, wkT, wvT)


def _mem_attn_kernel(xn_ref, wq_ref, kT_ref, vT_ref, o_ref):
    q = _dot(xn_ref[0], wq_ref[...]).astype(BF16)
    outs = []
    for h in range(H_MEM):
        hh = slice(h * HD_MEM, (h + 1) * HD_MEM)
        s = _dot(q[:, hh], kT_ref[0, hh, :].astype(BF16)) * (HD_MEM ** -0.5)
        p = jnp.exp(s - jnp.max(s, axis=1, keepdims=True))
        p = p / jnp.sum(p, axis=1, keepdims=True)
        outs.append(_dot_nt(p.astype(BF16), vT_ref[0, hh, :].astype(BF16)))
    o_ref[0] = jnp.concatenate(outs, axis=1).astype(o_ref.dtype)


def mem_attn(xn, wq, kT, vT, tl):
    B, L, D = xn.shape
    M = kT.shape[2]
    return pl.pallas_call(
        _mem_attn_kernel,
        grid=(B, L // tl),
        in_specs=[pl.BlockSpec((1, tl, D), lambda b, i: (b, i, 0)), _const_spec(wq.shape),
                  pl.BlockSpec((1, W_MEM, M), lambda b, i: (b, 0, 0)),
                  pl.BlockSpec((1, W_MEM, M), lambda b, i: (b, 0, 0))],
        out_specs=pl.BlockSpec((1, tl, W_MEM), lambda b, i: (b, i, 0)),
        out_shape=jax.ShapeDtypeStruct((B, L, W_MEM), BF16),
        compiler_params=_params(("parallel", "parallel")),
        name="mem_attn",
    )(xn, wq, kT, vT)


HOST_CHUNKS = 4


def _run_host(n_own_in, n_own_out, chunk_fn, finish_fn, refs, plan):
    if plan is None:
        for c in range(HOST_CHUNKS):
            chunk_fn(c)
        finish_fn()
        return
    assert plan.groups == HOST_CHUNKS
    pt_ref, refs = refs[0], refs[1:]
    dec_in = refs[n_own_in:n_own_in + 7]
    dec_out = refs[n_own_in + 7 + n_own_out]
    dec_scratch = refs[n_own_in + 7 + n_own_out + 1:]
    _decode_side_task(plan, pl.program_id(0), pl.num_programs(0),
                      (pt_ref, *dec_in, dec_out, *dec_scratch), chunk_fn)
    finish_fn()


def _host_call(kernel, name, n_rows, tm, own_in_specs, own_out_spec, own_out_shape, own_args, decode):
    if decode is None:
        return pl.pallas_call(
            functools.partial(kernel, plan=None), grid=(n_rows // tm,), in_specs=own_in_specs,
            out_specs=own_out_spec, out_shape=own_out_shape, compiler_params=_params(("parallel",)),
            name=name)(*own_args), None
    plan, page_table, dec_args = decode
    dec_in_specs, dec_out_spec, dec_out_shape, dec_scratch = _decode_host_specs(plan)
    grid_spec = pltpu.PrefetchScalarGridSpec(
        num_scalar_prefetch=1, grid=(n_rows // tm,),
        in_specs=list(own_in_specs) + dec_in_specs,
        out_specs=(own_out_spec, dec_out_spec),
        scratch_shapes=dec_scratch)
    return pl.pallas_call(
        functools.partial(kernel, plan=plan), grid_spec=grid_spec,
        out_shape=(own_out_shape, dec_out_shape),
        compiler_params=_params(("arbitrary",)), name=name + "_decode")(page_table, *own_args, *dec_args)


def _merge_kernel(*refs, plan):
    own = refs[1:] if plan is not None else refs
    x_ref, xn_ref, fo_ref, go_ref, mo_ref, wgl_ref, wuf_ref, wug_ref, wum_ref, wout_ref = own[:10]
    h_ref = own[10 + (7 if plan is not None else 0)]
    D = x_ref.shape[1]
    branches = ((fo_ref, wuf_ref), (go_ref, wug_ref), (mo_ref, wum_ref))
    merged = []

    def chunk(c):
        if c < N_BRANCH:
            br, w = branches[c]
            gate = _sigmoid(_dot(xn_ref[...], wgl_ref[:, c * D:(c + 1) * D]))
            merged.append(gate * _dot(br[...], w[...]))
        else:
            m = (merged[0] + merged[1] + merged[2]).astype(BF16)
            h_ref[...] = x_ref[...] + _dot(m, wout_ref[...])

    _run_host(10, 1, chunk, lambda: None, refs, plan)


def merge(x, xn, fo, go, mo, wgl, wuf, wug, wum, wout, tm, decode=None):
    N, D = x.shape
    row = lambda n: pl.BlockSpec((tm, n), lambda i, *_: (i, 0))
    consts = (wgl, wuf, wug, wum, wout)
    in_specs = [row(D), row(D), row(fo.shape[1]), row(go.shape[1]), row(mo.shape[1])] \
        + [_const_spec(w.shape) for w in consts]
    return _host_call(_merge_kernel, "merge", N, tm, in_specs, row(D), jax.ShapeDtypeStruct((N, D), F32),
                      (x, xn, fo, go, mo, *consts), decode)


def _mlp_kernel(*refs, plan):
    own = refs[1:] if plan is not None else refs
    h_ref, g_ref, w1_ref, w2_ref, gf_ref = own[:5]
    y_ref = own[5 + (7 if plan is not None else 0)]
    fc = w1_ref.shape[1] // HOST_CHUNKS
    state = {}

    def chunk(c):
        if c == 0:
            state["acc"] = h_ref[...]
            state["hn"] = _rmsnorm(state["acc"], g_ref[...]).astype(BF16)
        cc = slice(c * fc, (c + 1) * fc)
        u = jnp.maximum(_dot(state["hn"], w1_ref[:, cc]), 0.0)
        state["acc"] = state["acc"] + _dot((u * u).astype(BF16), w2_ref[cc, :])

    def finish():
        y_ref[...] = _rmsnorm(state["acc"], gf_ref[...])

    _run_host(5, 1, chunk, finish, refs, plan)


def mlp(h, g, w1, w2, gf, tm, decode=None):
    N, D = h.shape
    row = pl.BlockSpec((tm, D), lambda i, *_: (i, 0))
    in_specs = [row, _const_spec(g.shape), _const_spec(w1.shape), _const_spec(w2.shape), _const_spec(gf.shape)]
    return _host_call(_mlp_kernel, "mlp", N, tm, in_specs, row, jax.ShapeDtypeStruct((N, D), F32),
                      (h, g, w1, w2, gf), decode)


def _sample_proj_kernel(x_ref, g_ref, w_ref, xn_ref, o_ref):
    xn = _rmsnorm(x_ref[...], g_ref[...]).astype(BF16)
    xn_ref[...] = xn
    o_ref[...] = _dot(xn, w_ref[...])


def sample_proj(x, g, w, tn):
    N, D = x.shape
    NP = w.shape[1]
    return pl.pallas_call(
        _sample_proj_kernel,
        grid=(NP // tn,),
        in_specs=[_const_spec((N, D)), _const_spec(g.shape), pl.BlockSpec((D, tn), lambda j: (0, j))],
        out_specs=(_const_spec((N, D)), pl.BlockSpec((N, tn), lambda j: (0, j))),
        out_shape=(jax.ShapeDtypeStruct((N, D), BF16), jax.ShapeDtypeStruct((N, NP), F32)),
        compiler_params=_params(("arbitrary",)),
        name="sample_proj",
    )(x, g, w)


def _logsig_kernel(ff_ref, b_ref, o_ref):
    o_ref[...] = _log_sigmoid(ff_ref[...] + b_ref[...])


def logsig_bias(ff, b):
    return pl.pallas_call(_logsig_kernel, out_shape=jax.ShapeDtypeStruct(ff.shape, F32), name="logsig")(ff, b)


class _DecodePlan(NamedTuple):
    first_b: int
    n_b: int
    steps_per_b: int
    groups: int
    pages: int
    n_pages: int


def _plan_decode(first_b, n_b, n_steps, n_pages, pages):
    steps_per_b, rem = divmod(n_steps, n_b)
    assert rem == 0 and steps_per_b >= 1, (n_steps, n_b)
    per_step, rem = divmod(n_pages, steps_per_b)
    assert rem == 0, (n_pages, steps_per_b)
    groups, rem = divmod(per_step, pages)
    assert rem == 0 and groups % 2 == 0, (per_step, pages)
    return _DecodePlan(first_b, n_b, steps_per_b, groups, pages, n_pages)


def _decode_side_task(plan, i, n_steps, refs, host_chunk):
    (pt_ref, q_ref, kn_ref, vn_ref, lfn_ref, k_hbm, v_hbm, lf_hbm, o_ref,
     kbuf, vbuf, lfbuf, sem, qs_ref, m_ref, l_ref, r_ref, acc_ref) = refs
    G = plan.pages
    part = i % plan.steps_per_b
    lane = lax.broadcasted_iota(jnp.int32, (H_FOX, LANES), 1)
    lane_w = lax.broadcasted_iota(jnp.int32, (W_FOX, LANES), 1)

    def group_copies(step, g, slot):
        b = plan.first_b + step // plan.steps_per_b
        done = (step % plan.steps_per_b) * plan.groups * G + g * G
        copies = []
        for k in range(G):
            page = pt_ref[b, plan.n_pages - 1 - (done + k)]
            copies += [pltpu.make_async_copy(k_hbm.at[page], kbuf.at[slot, k], sem.at[slot, 0]),
                       pltpu.make_async_copy(v_hbm.at[page], vbuf.at[slot, k], sem.at[slot, 1]),
                       pltpu.make_async_copy(lf_hbm.at[page], lfbuf.at[slot, k], sem.at[slot, 2])]
        return copies

    def head_scores(k_ref):
        return jnp.concatenate(
            [jnp.sum(k_ref[h * HD_FOX:(h + 1) * HD_FOX, :] * qs_ref[h * HD_FOX:(h + 1) * HD_FOX, :],
                     axis=0, keepdims=True) for h in range(H_FOX)], axis=0)

    @pl.when(i == 0)
    def _():
        for c in group_copies(0, 0, 0):
            c.start()

    @pl.when(part == 0)
    def _():
        qs_ref[...] = q_ref[0] * (HD_FOX ** -0.5)
        m_ref[...] = head_scores(kn_ref.at[0])
        l_ref[...] = jnp.where(lane == 0, 1.0, 0.0)
        r_ref[...] = jnp.zeros_like(r_ref)
        acc_ref[...] = jnp.where(lane_w == 0, vn_ref[0], 0.0)

    for g in range(plan.groups):
        slot = g % 2
        if g + 1 < plan.groups:
            for c in group_copies(i, g + 1, 1 - slot):
                c.start()
        else:
            @pl.when(i + 1 < n_steps)
            def _():
                for c in group_copies(i + 1, 0, 1 - slot):
                    c.start()
        host_chunk(g)
        for c in group_copies(i, g, slot):
            c.wait()

        r_run = r_ref[...]
        scores = []
        for k in range(G):
            lf = lfbuf[slot, k]
            incl = _lane_cumsum(lf, reverse=True)
            scores.append(head_scores(kbuf.at[slot, k]) + lfn_ref[0] + (incl - lf + r_run))
            r_run = r_run + incl[:, 0:1]
        r_ref[...] = r_run
        m_prev = m_ref[...]
        m_new = m_prev
        for s in scores:
            m_new = jnp.maximum(m_new, jnp.max(s, axis=1, keepdims=True))
        alpha = jnp.exp(m_prev - m_new)
        probs = [jnp.exp(s - m_new) for s in scores]
        l_ref[...] = alpha * l_ref[...] + functools.reduce(lambda a, b: a + b, probs)
        m_ref[...] = m_new
        for h in range(H_FOX):
            rows = slice(h * HD_FOX, (h + 1) * HD_FOX)
            a = acc_ref[rows, :] * alpha[h:h + 1, :]
            for k in range(G):
                a = a + vbuf[slot, k, rows, :] * probs[k][h:h + 1, :]
            acc_ref[rows, :] = a

    @pl.when(part == plan.steps_per_b - 1)
    def _():
        inv = 1.0 / jnp.sum(l_ref[...], axis=1, keepdims=True)
        for h in range(H_FOX):
            rows = slice(h * HD_FOX, (h + 1) * HD_FOX)
            acc_ref[rows, :] = acc_ref[rows, :] * inv[h:h + 1, :]
        ones = jnp.ones((8, LANES), BF16)
        out = None
        for piece in _split_bf16(acc_ref[...], 3):
            t = _dot_nt(ones, piece)
            out = t if out is None else out + t
        o_ref[0] = out[0:1, :]


def _decode_host_specs(plan):
    b_of = lambda i: plan.first_b + i // plan.steps_per_b
    per_b = lambda n: pl.BlockSpec((1, n, LANES), lambda i, pt: (b_of(i), 0, 0))
    pool = pl.BlockSpec(memory_space=pl.ANY)
    in_specs = [per_b(W_FOX), per_b(W_FOX), per_b(W_FOX), per_b(H_FOX), pool, pool, pool]
    out_spec = pl.BlockSpec((1, 1, W_FOX), lambda i, pt: (i // plan.steps_per_b, 0, 0))
    out_shape = jax.ShapeDtypeStruct((plan.n_b, 1, W_FOX), F32)
    G = plan.pages
    scratch = [pltpu.VMEM((2, G, W_FOX, LANES), F32), pltpu.VMEM((2, G, W_FOX, LANES), F32),
               pltpu.VMEM((2, G, H_FOX, LANES), F32), pltpu.SemaphoreType.DMA((2, 3)),
               pltpu.VMEM((W_FOX, LANES), F32), pltpu.VMEM((H_FOX, LANES), F32),
               pltpu.VMEM((H_FOX, LANES), F32), pltpu.VMEM((H_FOX, LANES), F32),
               pltpu.VMEM((W_FOX, LANES), F32)]
    return in_specs, out_spec, out_shape, scratch


def _sample_gla_kernel(q_ref, k_ref, ga_ref, v_ref, gr_ref, s0_ref, wuT_ref, bcol_ref, g_ref, o_ref, st_ref):
    z = _dot(wuT_ref[...], ga_ref[0].astype(BF16)) + bcol_ref[...]
    la = _log_sigmoid(z) / GLA_NORMALIZER
    ea = jnp.exp(la)
    k = k_ref[0]
    qe = q_ref[0] * (DK_GLA ** -0.5) * ea
    kd = k * jnp.exp(-la)
    gr = gr_ref[0]
    gate = gr * _sigmoid(gr)
    wide = lambda a: jnp.concatenate([a, a], axis=1)
    for h in range(H_GLA):
        kk = slice(h * DK_GLA, (h + 1) * DK_GLA)
        vv = slice(h * DV_GLA, (h + 1) * DV_GLA)
        s_old = s0_ref[0, h]
        v_h = v_ref[0][:, vv]
        a = jnp.sum(qe[kk] * kd[kk], axis=0, keepdims=True)
        o = wide(a) * v_h + jnp.sum(wide(qe[kk]) * s_old, axis=0, keepdims=True)
        st_ref[0, h] = wide(ea[kk]) * s_old + wide(k[kk]) * v_h
        o_ref[0, :, vv] = _rmsnorm(o, g_ref[...]) * gate[:, vv]


def sample_gla(q_rep, k_rep, ga_rep, v, gr, s0, wuT, bcol, g):
    DB = q_rep.shape[0]
    per_b = lambda a: pl.BlockSpec((1,) + a.shape[1:], lambda b: (b,) + (0,) * (a.ndim - 1))
    return pl.pallas_call(
        _sample_gla_kernel,
        grid=(DB,),
        in_specs=[per_b(q_rep), per_b(k_rep), per_b(ga_rep), per_b(v), per_b(gr), per_b(s0),
                  _const_spec(wuT.shape), _const_spec(bcol.shape), _const_spec(g.shape)],
        out_specs=(per_b(v), per_b(s0)),
        out_shape=(jax.ShapeDtypeStruct(v.shape, F32), jax.ShapeDtypeStruct(s0.shape, F32)),
        compiler_params=_params(("parallel",)),
        name="sample_gla",
    )(q_rep, k_rep, ga_rep, v, gr, s0, wuT, bcol, g)


_SPLITS = (W_FOX, W_FOX, W_FOX, H_FOX, W_GLA_K, W_GLA_K, W_GLA_V, GLA_RANK, W_GLA_V, W_MEM)
PROMPT_TILE = 512
GLA_TILE = 256
ROW_TILE = 256
DECODE_PAGES = 4
SAMPLE_PROJ_TILE = 512
SAMPLE_MEM_ROWS = 8


def _lane_rep(a):
    return jnp.broadcast_to(a[..., None], a.shape + (LANES,))


def kernel(x_prompt, x_sample, mem_prompt, cache_fox_k, cache_fox_v, cache_fox_logf, state_gla, cache_mem_k, cache_mem_v, page_table, g_attn, w_in, b_fox_f, w_gla_gate_up, b_gla_gate, g_gla_norm, g_mem, w_mem_k, w_mem_v, w_up_fox, w_up_gla, w_up_mem, w_out, g_mlp, w_mlp_in, w_mlp_out, g_final):
    B, S, D = x_prompt.shape
    DB = x_sample.shape[0]
    assert w_in.shape[0] == 1 and x_sample.shape[1] == 1

    w = w_in[0]
    segs, off = [], 0
    for n in _SPLITS:
        segs.append(w[:, off:off + n])
        off += n
    w_fq, w_fk, w_fv, w_ff, w_gq, w_gk, w_gv, w_ga, w_gr, w_mq = segs
    w_gl = w[:, off:]
    bf = lambda a: a.astype(BF16)
    bfT = lambda a: a.T.astype(BF16)
    row = lambda a: a.reshape(1, -1)
    col = lambda a: a.reshape(-1, 1)

    ga_args = (bfT(w_gq), bfT(w_gk), bf(w_gv), bfT(w_ga), bf(w_gr), bfT(w_gla_gate_up[0]),
               col(b_gla_gate[0]), row(g_gla_norm[0]))
    merge_w = (bf(w_gl), bf(w_up_fox[0]), bf(w_up_gla[0]), bf(w_up_mem[0]), bf(w_out[0]))
    mlp_w = (row(g_mlp[0]), bf(w_mlp_in[0]), bf(w_mlp_out[0]), row(g_final))
    g_attn_row = row(g_attn[0])

    xs = x_sample.reshape(DB, D)
    d_in = w.shape[1]
    d_pad = -(-d_in // SAMPLE_PROJ_TILE) * SAMPLE_PROJ_TILE
    xn_s, proj = sample_proj(xs, g_attn_row, bf(jnp.pad(w, ((0, 0), (0, d_pad - d_in)))), SAMPLE_PROJ_TILE)
    parts, off = [], 0
    for n in _SPLITS[:-1]:
        parts.append(proj[:, off:off + n])
        off += n
    s_fq, s_fk, s_fv, s_ff, s_gq, s_gk, s_gv, s_ga, s_gr = parts
    lf_new = logsig_bias(s_ff, row(b_fox_f[0]))

    pool_T = lambda c: c.transpose(0, 2, 3, 1).reshape(c.shape[0], W_FOX, c.shape[1])
    decode_args = (_lane_rep(s_fq), _lane_rep(s_fk), _lane_rep(s_fv), _lane_rep(lf_new),
                   pool_T(cache_fox_k[0]), pool_T(cache_fox_v[0]), cache_fox_logf[0].transpose(0, 2, 1))

    xn, qT, ka, kT, vT, vT16, lfT, cT = fox_proj(
        x_prompt, g_attn_row, bfT(w_fq), bf(w_fk), bfT(w_fv), bfT(w_ff), col(b_fox_f[0]), PROMPT_TILE)
    fox_o = fox_attn(qT, cT, ka, vT16, PROMPT_TILE)
    gla_o, p_state = gla(xn, *ga_args, GLA_TILE)
    mkT, mvT = mem_kv(mem_prompt, row(g_mem[0]), bfT(w_mem_k[0]), bfT(w_mem_v[0]))
    mem_o = mem_attn(xn, bf(w_mq), mkT, mvT, PROMPT_TILE)
    N = B * S
    n_steps = N // ROW_TILE
    half = DB // 2
    n_pages = page_table.shape[1]
    plan_a = _plan_decode(0, half, n_steps, n_pages, DECODE_PAGES)
    plan_b = _plan_decode(half, DB - half, n_steps, n_pages, DECODE_PAGES)
    h, fox_o_a = merge(x_prompt.reshape(N, D), xn.reshape(N, D), fox_o.reshape(N, W_FOX),
                       gla_o.reshape(N, W_GLA_V), mem_o.reshape(N, W_MEM), *merge_w, ROW_TILE,
                       decode=(plan_a, page_table, decode_args))
    y_prompt, fox_o_b = mlp(h, *mlp_w, ROW_TILE, decode=(plan_b, page_table, decode_args))
    y_prompt = y_prompt.reshape(B, S, D)
    fox_o_s = jnp.concatenate([fox_o_a, fox_o_b], axis=0)

    heads_out = lambda aT, nh, hd: aT.reshape(aT.shape[0], nh, hd, aT.shape[2]).transpose(0, 3, 1, 2)[None]
    p_fox_k = heads_out(kT, H_FOX, HD_FOX)
    p_fox_v = heads_out(vT, H_FOX, HD_FOX)
    p_fox_logf = lfT.transpose(0, 2, 1)[None]
    p_mem_k = heads_out(mkT, H_MEM, HD_MEM)
    p_mem_v = heads_out(mvT, H_MEM, HD_MEM)

    gla_o_s, s_state = sample_gla(_lane_rep(s_gq), _lane_rep(s_gk), _lane_rep(s_ga), s_gv[:, None, :],
                                  s_gr[:, None, :], state_gla[0], bfT(w_gla_gate_up[0]), col(b_gla_gate[0]),
                                  row(g_gla_norm[0]))
    mem_T = lambda c: c.transpose(0, 2, 3, 1).reshape(c.shape[0], W_MEM, c.shape[1])
    xn_rows = jnp.broadcast_to(xn_s[:, None, :], (DB, SAMPLE_MEM_ROWS, D))
    mem_o_s = mem_attn(xn_rows, bf(w_mq), mem_T(cache_mem_k[0]), mem_T(cache_mem_v[0]), SAMPLE_MEM_ROWS)[:, 0]
    h_s, _ = merge(xs, xn_s, bf(fox_o_s.reshape(DB, W_FOX)), bf(gla_o_s.reshape(DB, W_GLA_V)), mem_o_s,
                   *merge_w, DB)
    y_sample, _ = mlp(h_s, *mlp_w, DB)
    y_sample = y_sample.reshape(DB, 1, D)

    s_fox_k = s_fk.reshape(1, DB, 1, H_FOX, HD_FOX)
    s_fox_v = s_fv.reshape(1, DB, 1, H_FOX, HD_FOX)
    s_fox_logf = lf_new.reshape(1, DB, 1, H_FOX)
    return (y_prompt, y_sample, p_fox_k, p_fox_v, p_fox_logf, p_state[None], p_mem_k, p_mem_v,
            s_fox_k, s_fox_v, s_fox_logf, s_state[None])
```

```python
import functools
from typing import NamedTuple

import jax
import jax.numpy as jnp
from jax import lax
from jax.experimental import pallas as pl
from jax.experimental.pallas import tpu as pltpu

F32 = jnp.float32
BF16 = jnp.bfloat16

H_FOX = 16
HD_FOX = 64
W_FOX = H_FOX * HD_FOX
H_GLA = 4
DK_GLA = 128
DV_GLA = 256
W_GLA_K = H_GLA * DK_GLA
W_GLA_V = H_GLA * DV_GLA
GLA_RANK = 16
GLA_NORMALIZER = 16.0
GLA_CHUNK = 64
MEM_LEN = 256
H_MEM = 4
HD_MEM = 64
W_MEM = H_MEM * HD_MEM
N_BRANCH = 3
EPS = 1e-6

LANES = 128
FOX_AUG = 128
VMEM_LIMIT = 56 * 1024 * 1024


def _dot(a, b):
    return jnp.dot(a, b, preferred_element_type=F32)


def _dot_nt(a, b):
    return lax.dot_general(a, b, (((1,), (1,)), ((), ())), preferred_element_type=F32)


def _split_bf16(x, parts):
    out = []
    r = x
    for _ in range(parts):
        p = r.astype(BF16)
        out.append(p)
        r = r - p.astype(F32)
    return out


def _log_sigmoid(x):
    return jnp.minimum(x, 0.0) - jnp.log1p(jnp.exp(-jnp.abs(x)))


def _sigmoid(x):
    return 1.0 / (1.0 + jnp.exp(-x))


def _rmsnorm(x, g):
    ms = jnp.mean(x * x, axis=-1, keepdims=True)
    return x * lax.rsqrt(ms + EPS) * g


def _lane_cumsum(x, reverse=False):
    rows, n = x.shape
    lane = lax.broadcasted_iota(jnp.int32, (rows, LANES), 1)
    blocks = []
    for i in range(n // LANES):
        y = x[:, i * LANES:(i + 1) * LANES]
        k = 1
        while k < LANES:
            if reverse:
                y = y + jnp.where(lane < LANES - k, pltpu.roll(y, LANES - k, 1), 0.0)
            else:
                y = y + jnp.where(lane >= k, pltpu.roll(y, k, 1), 0.0)
            k *= 2
        blocks.append(y)
    order = range(len(blocks) - 1, -1, -1) if reverse else range(len(blocks))
    edge = 0 if reverse else LANES - 1
    carry = None
    for i in order:
        if carry is not None:
            blocks[i] = blocks[i] + carry
        carry = blocks[i][:, edge:edge + 1]
    return blocks[0] if len(blocks) == 1 else jnp.concatenate(blocks, axis=1)


def _const_spec(shape):
    return pl.BlockSpec(shape, lambda *_: (0,) * len(shape))


def _params(sem):
    return pltpu.CompilerParams(dimension_semantics=sem, vmem_limit_bytes=VMEM_LIMIT)


def _split3_f32(x):
    hi = x.astype(BF16).astype(F32)
    r = x - hi
    mid = r.astype(BF16).astype(F32)
    lo = (r - mid).astype(BF16).astype(F32)
    return hi, mid, lo


def _fox_proj_kernel(x_ref, g_ref, wqT_ref, wk_ref, wvT_ref, wffT_ref, bcol_ref, e_ref,
                     xn_ref, qT_ref, ka_ref, kT_ref, vT_ref, vT16_ref, lfT_ref, cT_ref, carry_ref):
    i = pl.program_id(1)
    ts = x_ref.shape[1]

    @pl.when(i == 0)
    def _():
        carry_ref[...] = jnp.zeros_like(carry_ref)

    xn = _rmsnorm(x_ref[0], g_ref[...]).astype(BF16)
    xn_ref[0] = xn
    qT_ref[0] = (_dot_nt(wqT_ref[...], xn) * (HD_FOX ** -0.5)).astype(BF16)
    vT = _dot_nt(wvT_ref[...], xn)
    vT_ref[0] = vT
    vT16_ref[0] = vT.astype(BF16)
    k = _dot(xn, wk_ref[...])
    kT_ref[0] = k.T

    lfT = _log_sigmoid(_dot_nt(wffT_ref[...], xn) + bcol_ref[...])
    lfT_ref[0] = lfT
    cT = _lane_cumsum(lfT) + carry_ref[...]
    carry_ref[...] = cT[:, ts - 1:ts]
    cT_ref[0] = cT

    hi, mid, lo = _split3_f32(-cT)
    stack = jnp.concatenate([hi, mid, lo, jnp.ones((H_FOX, ts), F32),
                             jnp.zeros((LANES - 4 * H_FOX, ts), F32)], axis=0)
    ex = _dot(stack.T.astype(BF16), e_ref[...])
    lower = lax.broadcasted_iota(jnp.int32, (ts, LANES), 1) < HD_FOX
    for g in range(H_FOX // 2):
        kg = k[:, g * LANES:(g + 1) * LANES]
        eg = ex[:, g * LANES:(g + 1) * LANES]
        ka_ref[0, :, (2 * g) * FOX_AUG:(2 * g + 1) * FOX_AUG] = jnp.where(lower, kg, eg).astype(BF16)
        ka_ref[0, :, (2 * g + 1) * FOX_AUG:(2 * g + 2) * FOX_AUG] = jnp.where(lower, eg, kg).astype(BF16)


def _fox_bias_selector():
    e = jnp.zeros((LANES, W_FOX), F32)
    for h in range(H_FOX):
        base = (h // 2) * LANES + (HD_FOX if h % 2 == 0 else 0)
        for j in range(3):
            e = e.at[3 * H_FOX, base + j].set(1.0)
            e = e.at[j * H_FOX + h, base + 3 + j].set(1.0)
    return e.astype(BF16)


def fox_proj(x, g, wqT, wk, wvT, wffT, bcol, ts):
    B, S, D = x.shape
    e = _fox_bias_selector()
    out_shape = (
        jax.ShapeDtypeStruct((B, S, D), BF16),
        jax.ShapeDtypeStruct((B, W_FOX, S), BF16),
        jax.ShapeDtypeStruct((B, S, H_FOX * FOX_AUG), BF16),
        jax.ShapeDtypeStruct((B, W_FOX, S), F32),
        jax.ShapeDtypeStruct((B, W_FOX, S), F32),
        jax.ShapeDtypeStruct((B, W_FOX, S), BF16),
        jax.ShapeDtypeStruct((B, H_FOX, S), F32),
        jax.ShapeDtypeStruct((B, H_FOX, S), F32),
    )
    tile = lambda n: pl.BlockSpec((1, ts, n), lambda b, i: (b, i, 0))
    tileT = lambda n: pl.BlockSpec((1, n, ts), lambda b, i: (b, 0, i))
    consts = (g, wqT, wk, wvT, wffT, bcol, e)
    return pl.pallas_call(
        _fox_proj_kernel,
        grid=(B, S // ts),
        in_specs=[tile(D)] + [_const_spec(c.shape) for c in consts],
        out_specs=(tile(D), tileT(W_FOX), tile(H_FOX * FOX_AUG), tileT(W_FOX), tileT(W_FOX),
                   tileT(W_FOX), tileT(H_FOX), tileT(H_FOX)),
        out_shape=out_shape,
        scratch_shapes=[pltpu.VMEM((H_FOX, 1), F32)],
        compiler_params=_params(("parallel", "arbitrary")),
        name="fox_proj",
    )(x, *consts)


L_ROWS = 16
FOX_HEAD_GROUP = 4


def _fox_attn_kernel(qi_ref, kj_ref, qT_ref, cT_ref, ka_ref, vT_ref, o_ref,
                     qa_ref, s_ref, mx_ref, m_ref, acc_ref, *, tq):
    group = pl.program_id(1)
    n_heads = qa_ref.shape[0]
    S = qT_ref.shape[2]
    tk = tq
    n_pairs = qi_ref.shape[0]

    row8 = lax.broadcasted_iota(jnp.int32, (8, S), 0)
    for h in range(n_heads):
        hi, mid, lo = _split3_f32(cT_ref[0, pl.ds(n_heads * group + h, 1), :])
        ext8 = jnp.where(row8 == 0, hi, jnp.where(row8 == 1, mid, jnp.where(
            row8 == 2, lo, jnp.where(row8 < 6, 1.0, 0.0))))
        ext = jnp.concatenate([ext8, jnp.zeros((HD_FOX - 8, S), F32)], axis=0).astype(BF16)
        qh = qT_ref[0, h * HD_FOX:(h + 1) * HD_FOX, :]
        qa_ref[h] = jnp.concatenate([qh, ext] if h % 2 == 0 else [ext, qh], axis=0)
        m_ref[h] = jnp.full((1, tq), -jnp.inf, F32)
        acc_ref[h] = jnp.zeros((HD_FOX + L_ROWS, tq), F32)

    krow = lax.broadcasted_iota(jnp.int32, (tk, tq), 0)
    qcol = lax.broadcasted_iota(jnp.int32, (tk, tq), 1)
    causal = krow <= qcol
    ones = jnp.ones((L_ROWS, tk), BF16)

    def scores(t, slot, h):
        qoff = pl.multiple_of(qi_ref[t] * tq, tq)
        koff = pl.multiple_of(kj_ref[t] * tk, tk)
        sT = _dot(ka_ref[0, pl.ds(koff, tk), h * FOX_AUG:(h + 1) * FOX_AUG],
                  qa_ref[h, :, pl.ds(qoff, tq)])
        s_ref[slot, h] = sT
        mx_ref[slot, h] = jnp.max(sT, axis=0, keepdims=True)

    def consume(t, slot, h, diagonal):
        koff = pl.multiple_of(kj_ref[t] * tk, tk)
        sT = s_ref[slot, h]
        if diagonal:
            sT = jnp.where(causal, sT, -jnp.inf)
            mx = jnp.max(sT, axis=0, keepdims=True)
        else:
            mx = mx_ref[slot, h]
        m_prev = m_ref[h]
        m_next = jnp.maximum(m_prev, mx)
        pT = jnp.exp(sT - m_next).astype(BF16)
        alpha = jnp.exp(m_prev - m_next)
        va = jnp.concatenate([vT_ref[0, h * HD_FOX:(h + 1) * HD_FOX, pl.ds(koff, tk)], ones], axis=0)
        acc = alpha * acc_ref[h] + _dot(va, pT)
        if not diagonal:
            m_ref[h] = m_next
            acc_ref[h] = acc
            return None
        m_ref[h] = jnp.full((1, tq), -jnp.inf, F32)
        acc_ref[h] = jnp.zeros((HD_FOX + L_ROWS, tq), F32)
        return acc[:HD_FOX] / acc[HD_FOX:HD_FOX + 1]

    def trip(t, slot, diagonal):
        nxt = jnp.minimum(t + 1, n_pairs - 1)
        outs = []
        for h in range(n_heads):
            scores(nxt, 1 - slot, h)
            outs.append(consume(t, slot, h, diagonal))
        if diagonal:
            qoff = pl.multiple_of(qi_ref[t] * tq, tq)
            o_ref[0, pl.ds(qoff, tq), :] = jnp.concatenate(outs, axis=0).T.astype(o_ref.dtype)

    def either(t, slot):
        is_diag = kj_ref[t] == qi_ref[t]

        @pl.when(is_diag)
        def _():
            trip(t, slot, True)

        @pl.when(jnp.logical_not(is_diag))
        def _():
            trip(t, slot, False)

    for h in range(n_heads):
        scores(0, 0, h)

    def body(u, carry):
        either(2 * u, 0)
        either(2 * u + 1, 1)
        return carry

    lax.fori_loop(0, n_pairs // 2, body, 0)
    if n_pairs % 2:
        either(n_pairs - 1, 0)


def fox_attn(qT, cT, ka, vT16, tq):
    B, _, S = qT.shape
    nq = S // tq
    qi = jnp.asarray([i for i in range(nq) for _ in range(i + 1)], jnp.int32)
    kj = jnp.asarray([j for i in range(nq) for j in range(i + 1)], jnp.int32)
    hg = FOX_HEAD_GROUP
    grid_spec = pltpu.PrefetchScalarGridSpec(
        num_scalar_prefetch=2,
        grid=(B, H_FOX // hg),
        in_specs=[pl.BlockSpec((1, hg * HD_FOX, S), lambda b, p, *_: (b, p, 0)),
                  pl.BlockSpec((1, H_FOX, S), lambda b, p, *_: (b, 0, 0)),
                  pl.BlockSpec((1, S, hg * FOX_AUG), lambda b, p, *_: (b, 0, p)),
                  pl.BlockSpec((1, hg * HD_FOX, S), lambda b, p, *_: (b, p, 0))],
        out_specs=pl.BlockSpec((1, S, hg * HD_FOX), lambda b, p, *_: (b, 0, p)),
        scratch_shapes=[pltpu.VMEM((hg, FOX_AUG, S), BF16),
                        pltpu.VMEM((2, hg, tq, tq), F32),
                        pltpu.VMEM((2, hg, 1, tq), F32),
                        pltpu.VMEM((hg, 1, tq), F32),
                        pltpu.VMEM((hg, HD_FOX + L_ROWS, tq), F32)],
    )
    return pl.pallas_call(
        functools.partial(_fox_attn_kernel, tq=tq),
        grid_spec=grid_spec,
        out_shape=jax.ShapeDtypeStruct((B, S, W_FOX), BF16),
        compiler_params=_params(("parallel", "parallel")),
        name="fox_attn",
    )(qi, kj, qT, cT, ka, vT16)


def _gla_kernel(xn_ref, wqT_ref, wkT_ref, wv_ref, waT_ref, wr_ref, wuT_ref, bcol_ref, g_ref,
                o_ref, st_ref, s_ref, oacc_ref):
    t = pl.program_id(1)
    T = xn_ref.shape[1]
    C = GLA_CHUNK

    @pl.when(t == 0)
    def _():
        s_ref[...] = jnp.zeros_like(s_ref)

    xn = xn_ref[0]
    qT = _dot_nt(wqT_ref[...], xn) * (DK_GLA ** -0.5)
    kT = _dot_nt(wkT_ref[...], xn)
    v = _dot(xn, wv_ref[...]).astype(BF16)
    gaT = _dot_nt(waT_ref[...], xn).astype(BF16)
    laT = _log_sigmoid(_dot(wuT_ref[...], gaT) + bcol_ref[...]) / GLA_NORMALIZER

    r = lax.broadcasted_iota(jnp.int32, (T, T), 0)
    c_ = lax.broadcasted_iota(jnp.int32, (T, T), 1)
    same = (r // C) == (c_ // C)
    upper = jnp.logical_and(same, r <= c_).astype(BF16)
    block = same.astype(BF16)
    bT = totT = None
    for part in _split_bf16(laT, 2):
        d = _dot(part, upper)
        bT = d if bT is None else bT + d
        d = _dot(part, block)
        totT = d if totT is None else totT + d

    qe = (qT * jnp.exp(bT)).T.astype(BF16)
    kdT = (kT * jnp.exp(-bT)).astype(BF16)
    keT = (kT * jnp.exp(totT - bT)).astype(BF16)
    decT = jnp.exp(totT)
    in_chunk_causal = jnp.logical_and(same, c_ <= r)

    n_chunks = T // C
    for h in range(H_GLA):
        kk = slice(h * DK_GLA, (h + 1) * DK_GLA)
        vv = slice(h * DV_GLA, (h + 1) * DV_GLA)
        a = jnp.where(in_chunk_causal, _dot(qe[:, kk], kdT[kk, :]), 0.0)
        o_intra = _dot(a.astype(BF16), v[:, vv])
        ds = [_dot(keT[kk, c * C:(c + 1) * C], v[c * C:(c + 1) * C, vv]) for c in range(n_chunks)]
        s = s_ref[h]
        for c in range(n_chunks):
            rows = slice(c * C, (c + 1) * C)
            oacc_ref[rows, vv] = o_intra[rows] + _dot(qe[rows, kk], s.astype(BF16))
            s = decT[kk, c * C:c * C + 1] * s + ds[c]
        s_ref[h] = s

    gr = _dot(xn, wr_ref[...])
    gate = gr * _sigmoid(gr)
    for h in range(H_GLA):
        vv = slice(h * DV_GLA, (h + 1) * DV_GLA)
        y = _rmsnorm(oacc_ref[:, vv], g_ref[...])
        o_ref[0, :, vv] = (y * gate[:, vv]).astype(o_ref.dtype)

    @pl.when(t == pl.num_programs(1) - 1)
    def _():
        st_ref[0] = s_ref[...]


def gla(xn, wqT, wkT, wv, waT, wr, wuT, bcol, g, T):
    B, S, D = xn.shape
    consts = (wqT, wkT, wv, waT, wr, wuT, bcol, g)
    return pl.pallas_call(
        _gla_kernel,
        grid=(B, S // T),
        in_specs=[pl.BlockSpec((1, T, D), lambda b, t: (b, t, 0))] + [_const_spec(w.shape) for w in consts],
        out_specs=(pl.BlockSpec((1, T, W_GLA_V), lambda b, t: (b, t, 0)),
                   pl.BlockSpec((1, H_GLA, DK_GLA, DV_GLA), lambda b, t: (b, 0, 0, 0))),
        out_shape=(jax.ShapeDtypeStruct((B, S, W_GLA_V), BF16),
                   jax.ShapeDtypeStruct((B, H_GLA, DK_GLA, DV_GLA), F32)),
        scratch_shapes=[pltpu.VMEM((H_GLA, DK_GLA, DV_GLA), F32), pltpu.VMEM((T, W_GLA_V), F32)],
        compiler_params=_params(("parallel", "arbitrary")),
        name="gla",
    )(xn, *consts)


def _mem_kv_kernel(m_ref, g_ref, wkT_ref, wvT_ref, kT_ref, vT_ref):
    mn = _rmsnorm(m_ref[0], g_ref[...]).astype(BF16)
    kT_ref[0] = _dot_nt(wkT_ref[...], mn)
    vT_ref[0] = _dot_nt(wvT_ref[...], mn)


def mem_kv(mem, g, wkT, wvT):
    B, M, D = mem.shape
    blk = pl.BlockSpec((1, W_MEM, M), lambda b: (b, 0, 0))
    return pl.pallas_call(
        _mem_kv_kernel,
        grid=(B,),
        in_specs=[pl.BlockSpec((1, M, D), lambda b: (b, 0, 0)), _const_spec(g.shape),
                  _const_spec(wkT.shape), _const_spec(wvT.shape)],
        out_specs=(blk, blk),
        out_shape=(jax.ShapeDtypeStruct((B, W_MEM, M), F32),) * 2,
        compiler_params=_params(("parallel",)),
        name="mem_kv",
    )(mem, g, wkT, wvT)


def _mem_attn_kernel(xn_ref, wq_ref, kT_ref, vT_ref, o_ref):
    q = _dot(xn_ref[0], wq_ref[...]).astype(BF16)
    outs = []
    for h in range(H_MEM):
        hh = slice(h * HD_MEM, (h + 1) * HD_MEM)
        s = _dot(q[:, hh], kT_ref[0, hh, :].astype(BF16)) * (HD_MEM ** -0.5)
        p = jnp.exp(s - jnp.max(s, axis=1, keepdims=True))
        p = p / jnp.sum(p, axis=1, keepdims=True)
        outs.append(_dot_nt(p.astype(BF16), vT_ref[0, hh, :].astype(BF16)))
    o_ref[0] = jnp.concatenate(outs, axis=1).astype(o_ref.dtype)


def mem_attn(xn, wq, kT, vT, tl):
    B, L, D = xn.shape
    M = kT.shape[2]
    return pl.pallas_call(
        _mem_attn_kernel,
        grid=(B, L // tl),
        in_specs=[pl.BlockSpec((1, tl, D), lambda b, i: (b, i, 0)), _const_spec(wq.shape),
                  pl.BlockSpec((1, W_MEM, M), lambda b, i: (b, 0, 0)),
                  pl.BlockSpec((1, W_MEM, M), lambda b, i: (b, 0, 0))],
        out_specs=pl.BlockSpec((1, tl, W_MEM), lambda b, i: (b, i, 0)),
        out_shape=jax.ShapeDtypeStruct((B, L, W_MEM), BF16),
        compiler_params=_params(("parallel", "parallel")),
        name="mem_attn",
    )(xn, wq, kT, vT)


HOST_CHUNKS = 4


def _run_host(n_own_in, n_own_out, chunk_fn, finish_fn, refs, plan):
    if plan is None:
        for c in range(HOST_CHUNKS):
            chunk_fn(c)
        finish_fn()
        return
    assert plan.groups == HOST_CHUNKS
    pt_ref, refs = refs[0], refs[1:]
    dec_in = refs[n_own_in:n_own_in + 7]
    dec_out = refs[n_own_in + 7 + n_own_out]
    dec_scratch = refs[n_own_in + 7 + n_own_out + 1:]
    _decode_side_task(plan, pl.program_id(0), pl.num_programs(0),
                      (pt_ref, *dec_in, dec_out, *dec_scratch), chunk_fn)
    finish_fn()


def _host_call(kernel, name, n_rows, tm, own_in_specs, own_out_spec, own_out_shape, own_args, decode):
    if decode is None:
        return pl.pallas_call(
            functools.partial(kernel, plan=None), grid=(n_rows // tm,), in_specs=own_in_specs,
            out_specs=own_out_spec, out_shape=own_out_shape, compiler_params=_params(("parallel",)),
            name=name)(*own_args), None
    plan, page_table, dec_args = decode
    dec_in_specs, dec_out_spec, dec_out_shape, dec_scratch = _decode_host_specs(plan)
    grid_spec = pltpu.PrefetchScalarGridSpec(
        num_scalar_prefetch=1, grid=(n_rows // tm,),
        in_specs=list(own_in_specs) + dec_in_specs,
        out_specs=(own_out_spec, dec_out_spec),
        scratch_shapes=dec_scratch)
    return pl.pallas_call(
        functools.partial(kernel, plan=plan), grid_spec=grid_spec,
        out_shape=(own_out_shape, dec_out_shape),
        compiler_params=_params(("arbitrary",)), name=name + "_decode")(page_table, *own_args, *dec_args)


def _merge_kernel(*refs, plan):
    own = refs[1:] if plan is not None else refs
    x_ref, xn_ref, fo_ref, go_ref, mo_ref, wgl_ref, wuf_ref, wug_ref, wum_ref, wout_ref = own[:10]
    h_ref = own[10 + (7 if plan is not None else 0)]
    D = x_ref.shape[1]
    branches = ((fo_ref, wuf_ref), (go_ref, wug_ref), (mo_ref, wum_ref))
    merged = []

    def chunk(c):
        if c < N_BRANCH:
            br, w = branches[c]
            gate = _sigmoid(_dot(xn_ref[...], wgl_ref[:, c * D:(c + 1) * D]))
            merged.append(gate * _dot(br[...], w[...]))
        else:
            m = (merged[0] + merged[1] + merged[2]).astype(BF16)
            h_ref[...] = x_ref[...] + _dot(m, wout_ref[...])

    _run_host(10, 1, chunk, lambda: None, refs, plan)


def merge(x, xn, fo, go, mo, wgl, wuf, wug, wum, wout, tm, decode=None):
    N, D = x.shape
    row = lambda n: pl.BlockSpec((tm, n), lambda i, *_: (i, 0))
    consts = (wgl, wuf, wug, wum, wout)
    in_specs = [row(D), row(D), row(fo.shape[1]), row(go.shape[1]), row(mo.shape[1])] \
        + [_const_spec(w.shape) for w in consts]
    return _host_call(_merge_kernel, "merge", N, tm, in_specs, row(D), jax.ShapeDtypeStruct((N, D), F32),
                      (x, xn, fo, go, mo, *consts), decode)


def _mlp_kernel(*refs, plan):
    own = refs[1:] if plan is not None else refs
    h_ref, g_ref, w1_ref, w2_ref, gf_ref = own[:5]
    y_ref = own[5 + (7 if plan is not None else 0)]
    fc = w1_ref.shape[1] // HOST_CHUNKS
    state = {}

    def chunk(c):
        if c == 0:
            state["acc"] = h_ref[...]
            state["hn"] = _rmsnorm(state["acc"], g_ref[...]).astype(BF16)
        cc = slice(c * fc, (c + 1) * fc)
        u = jnp.maximum(_dot(state["hn"], w1_ref[:, cc]), 0.0)
        state["acc"] = state["acc"] + _dot((u * u).astype(BF16), w2_ref[cc, :])

    def finish():
        y_ref[...] = _rmsnorm(state["acc"], gf_ref[...])

    _run_host(5, 1, chunk, finish, refs, plan)


def mlp(h, g, w1, w2, gf, tm, decode=None):
    N, D = h.shape
    row = pl.BlockSpec((tm, D), lambda i, *_: (i, 0))
    in_specs = [row, _const_spec(g.shape), _const_spec(w1.shape), _const_spec(w2.shape), _const_spec(gf.shape)]
    return _host_call(_mlp_kernel, "mlp", N, tm, in_specs, row, jax.ShapeDtypeStruct((N, D), F32),
                      (h, g, w1, w2, gf), decode)


def _sample_proj_kernel(x_ref, g_ref, w_ref, xn_ref, o_ref):
    xn = _rmsnorm(x_ref[...], g_ref[...]).astype(BF16)
    xn_ref[...] = xn
    o_ref[...] = _dot(xn, w_ref[...])


def sample_proj(x, g, w, tn):
    N, D = x.shape
    NP = w.shape[1]
    return pl.pallas_call(
        _sample_proj_kernel,
        grid=(NP // tn,),
        in_specs=[_const_spec((N, D)), _const_spec(g.shape), pl.BlockSpec((D, tn), lambda j: (0, j))],
        out_specs=(_const_spec((N, D)), pl.BlockSpec((N, tn), lambda j: (0, j))),
        out_shape=(jax.ShapeDtypeStruct((N, D), BF16), jax.ShapeDtypeStruct((N, NP), F32)),
        compiler_params=_params(("arbitrary",)),
        name="sample_proj",
    )(x, g, w)


def _logsig_kernel(ff_ref, b_ref, o_ref):
    o_ref[...] = _log_sigmoid(ff_ref[...] + b_ref[...])


def logsig_bias(ff, b):
    return pl.pallas_call(_logsig_kernel, out_shape=jax.ShapeDtypeStruct(ff.shape, F32), name="logsig")(ff, b)


class _DecodePlan(NamedTuple):
    first_b: int
    n_b: int
    steps_per_b: int
    groups: int
    pages: int
    n_pages: int


def _plan_decode(first_b, n_b, n_steps, n_pages, pages):
    steps_per_b, rem = divmod(n_steps, n_b)
    assert rem == 0 and steps_per_b >= 1, (n_steps, n_b)
    per_step, rem = divmod(n_pages, steps_per_b)
    assert rem == 0, (n_pages, steps_per_b)
    groups, rem = divmod(per_step, pages)
    assert rem == 0 and groups % 2 == 0, (per_step, pages)
    return _DecodePlan(first_b, n_b, steps_per_b, groups, pages, n_pages)


def _decode_side_task(plan, i, n_steps, refs, host_chunk):
    (pt_ref, q_ref, kn_ref, vn_ref, lfn_ref, k_hbm, v_hbm, lf_hbm, o_ref,
     kbuf, vbuf, lfbuf, sem, qs_ref, m_ref, l_ref, r_ref, acc_ref) = refs
    G = plan.pages
    part = i % plan.steps_per_b
    lane = lax.broadcasted_iota(jnp.int32, (H_FOX, LANES), 1)
    lane_w = lax.broadcasted_iota(jnp.int32, (W_FOX, LANES), 1)

    def group_copies(step, g, slot):
        b = plan.first_b + step // plan.steps_per_b
        done = (step % plan.steps_per_b) * plan.groups * G + g * G
        copies = []
        for k in range(G):
            page = pt_ref[b, plan.n_pages - 1 - (done + k)]
            copies += [pltpu.make_async_copy(k_hbm.at[page], kbuf.at[slot, k], sem.at[slot, 0]),
                       pltpu.make_async_copy(v_hbm.at[page], vbuf.at[slot, k], sem.at[slot, 1]),
                       pltpu.make_async_copy(lf_hbm.at[page], lfbuf.at[slot, k], sem.at[slot, 2])]
        return copies

    def head_scores(k_ref):
        return jnp.concatenate(
            [jnp.sum(k_ref[h * HD_FOX:(h + 1) * HD_FOX, :] * qs_ref[h * HD_FOX:(h + 1) * HD_FOX, :],
                     axis=0, keepdims=True) for h in range(H_FOX)], axis=0)

    @pl.when(i == 0)
    def _():
        for c in group_copies(0, 0, 0):
            c.start()

    @pl.when(part == 0)
    def _():
        qs_ref[...] = q_ref[0] * (HD_FOX ** -0.5)
        m_ref[...] = head_scores(kn_ref.at[0])
        l_ref[...] = jnp.where(lane == 0, 1.0, 0.0)
        r_ref[...] = jnp.zeros_like(r_ref)
        acc_ref[...] = jnp.where(lane_w == 0, vn_ref[0], 0.0)

    for g in range(plan.groups):
        slot = g % 2
        for c in group_copies(i, g, slot):
            c.wait()
        if g + 1 < plan.groups:
            for c in group_copies(i, g + 1, 1 - slot):
                c.start()
        else:
            @pl.when(i + 1 < n_steps)
            def _():
                for c in group_copies(i + 1, 0, 1 - slot):
                    c.start()
        host_chunk(g)

        r_run = r_ref[...]
        scores = []
        for k in range(G):
            lf = lfbuf[slot, k]
            incl = _lane_cumsum(lf, reverse=True)
            scores.append(head_scores(kbuf.at[slot, k]) + lfn_ref[0] + (incl - lf + r_run))
            r_run = r_run + incl[:, 0:1]
        r_ref[...] = r_run
        m_prev = m_ref[...]
        m_new = m_prev
        for s in scores:
            m_new = jnp.maximum(m_new, jnp.max(s, axis=1, keepdims=True))
        alpha = jnp.exp(m_prev - m_new)
        probs = [jnp.exp(s - m_new) for s in scores]
        l_ref[...] = alpha * l_ref[...] + functools.reduce(lambda a, b: a + b, probs)
        m_ref[...] = m_new
        for h in range(H_FOX):
            rows = slice(h * HD_FOX, (h + 1) * HD_FOX)
            a = acc_ref[rows, :] * alpha[h:h + 1, :]
            for k in range(G):
                a = a + vbuf[slot, k, rows, :] * probs[k][h:h + 1, :]
            acc_ref[rows, :] = a

    @pl.when(part == plan.steps_per_b - 1)
    def _():
        inv = 1.0 / jnp.sum(l_ref[...], axis=1, keepdims=True)
        for h in range(H_FOX):
            rows = slice(h * HD_FOX, (h + 1) * HD_FOX)
            acc_ref[rows, :] = acc_ref[rows, :] * inv[h:h + 1, :]
        ones = jnp.ones((8, LANES), BF16)
        out = None
        for piece in _split_bf16(acc_ref[...], 3):
            t = _dot_nt(ones, piece)
            out = t if out is None else out + t
        o_ref[0] = out[0:1, :]


def _decode_host_specs(plan):
    b_of = lambda i: plan.first_b + i // plan.steps_per_b
    per_b = lambda n: pl.BlockSpec((1, n, LANES), lambda i, pt: (b_of(i), 0, 0))
    pool = pl.BlockSpec(memory_space=pl.ANY)
    in_specs = [per_b(W_FOX), per_b(W_FOX), per_b(W_FOX), per_b(H_FOX), pool, pool, pool]
    out_spec = pl.BlockSpec((1, 1, W_FOX), lambda i, pt: (i // plan.steps_per_b, 0, 0))
    out_shape = jax.ShapeDtypeStruct((plan.n_b, 1, W_FOX), F32)
    G = plan.pages
    scratch = [pltpu.VMEM((2, G, W_FOX, LANES), F32), pltpu.VMEM((2, G, W_FOX, LANES), F32),
               pltpu.VMEM((2, G, H_FOX, LANES), F32), pltpu.SemaphoreType.DMA((2, 3)),
               pltpu.VMEM((W_FOX, LANES), F32), pltpu.VMEM((H_FOX, LANES), F32),
               pltpu.VMEM((H_FOX, LANES), F32), pltpu.VMEM((H_FOX, LANES), F32),
               pltpu.VMEM((W_FOX, LANES), F32)]
    return in_specs, out_spec, out_shape, scratch


def _sample_gla_kernel(q_ref, k_ref, ga_ref, v_ref, gr_ref, s0_ref, wuT_ref, bcol_ref, g_ref, o_ref, st_ref):
    z = _dot(wuT_ref[...], ga_ref[0].astype(BF16)) + bcol_ref[...]
    la = _log_sigmoid(z) / GLA_NORMALIZER
    ea = jnp.exp(la)
    k = k_ref[0]
    qe = q_ref[0] * (DK_GLA ** -0.5) * ea
    kd = k * jnp.exp(-la)
    gr = gr_ref[0]
    gate = gr * _sigmoid(gr)
    wide = lambda a: jnp.concatenate([a, a], axis=1)
    for h in range(H_GLA):
        kk = slice(h * DK_GLA, (h + 1) * DK_GLA)
        vv = slice(h * DV_GLA, (h + 1) * DV_GLA)
        s_old = s0_ref[0, h]
        v_h = v_ref[0][:, vv]
        a = jnp.sum(qe[kk] * kd[kk], axis=0, keepdims=True)
        o = wide(a) * v_h + jnp.sum(wide(qe[kk]) * s_old, axis=0, keepdims=True)
        st_ref[0, h] = wide(ea[kk]) * s_old + wide(k[kk]) * v_h
        o_ref[0, :, vv] = _rmsnorm(o, g_ref[...]) * gate[:, vv]


def sample_gla(q_rep, k_rep, ga_rep, v, gr, s0, wuT, bcol, g):
    DB = q_rep.shape[0]
    per_b = lambda a: pl.BlockSpec((1,) + a.shape[1:], lambda b: (b,) + (0,) * (a.ndim - 1))
    return pl.pallas_call(
        _sample_gla_kernel,
        grid=(DB,),
        in_specs=[per_b(q_rep), per_b(k_rep), per_b(ga_rep), per_b(v), per_b(gr), per_b(s0),
                  _const_spec(wuT.shape), _const_spec(bcol.shape), _const_spec(g.shape)],
        out_specs=(per_b(v), per_b(s0)),
        out_shape=(jax.ShapeDtypeStruct(v.shape, F32), jax.ShapeDtypeStruct(s0.shape, F32)),
        compiler_params=_params(("parallel",)),
        name="sample_gla",
    )(q_rep, k_rep, ga_rep, v, gr, s0, wuT, bcol, g)


_SPLITS = (W_FOX, W_FOX, W_FOX, H_FOX, W_GLA_K, W_GLA_K, W_GLA_V, GLA_RANK, W_GLA_V, W_MEM)
PROMPT_TILE = 512
GLA_TILE = 256
ROW_TILE = 256
DECODE_PAGES = 4
SAMPLE_PROJ_TILE = 512
SAMPLE_MEM_ROWS = 8


def _lane_rep(a):
    return jnp.broadcast_to(a[..., None], a.shape + (LANES,))


def kernel(x_prompt, x_sample, mem_prompt, cache_fox_k, cache_fox_v, cache_fox_logf, state_gla, cache_mem_k, cache_mem_v, page_table, g_attn, w_in, b_fox_f, w_gla_gate_up, b_gla_gate, g_gla_norm, g_mem, w_mem_k, w_mem_v, w_up_fox, w_up_gla, w_up_mem, w_out, g_mlp, w_mlp_in, w_mlp_out, g_final):
    B, S, D = x_prompt.shape
    DB = x_sample.shape[0]
    assert w_in.shape[0] == 1 and x_sample.shape[1] == 1

    w = w_in[0]
    segs, off = [], 0
    for n in _SPLITS:
        segs.append(w[:, off:off + n])
        off += n
    w_fq, w_fk, w_fv, w_ff, w_gq, w_gk, w_gv, w_ga, w_gr, w_mq = segs
    w_gl = w[:, off:]
    bf = lambda a: a.astype(BF16)
    bfT = lambda a: a.T.astype(BF16)
    row = lambda a: a.reshape(1, -1)
    col = lambda a: a.reshape(-1, 1)

    ga_args = (bfT(w_gq), bfT(w_gk), bf(w_gv), bfT(w_ga), bf(w_gr), bfT(w_gla_gate_up[0]),
               col(b_gla_gate[0]), row(g_gla_norm[0]))
    merge_w = (bf(w_gl), bf(w_up_fox[0]), bf(w_up_gla[0]), bf(w_up_mem[0]), bf(w_out[0]))
    mlp_w = (row(g_mlp[0]), bf(w_mlp_in[0]), bf(w_mlp_out[0]), row(g_final))
    g_attn_row = row(g_attn[0])

    xs = x_sample.reshape(DB, D)
    d_in = w.shape[1]
    d_pad = -(-d_in // SAMPLE_PROJ_TILE) * SAMPLE_PROJ_TILE
    xn_s, proj = sample_proj(xs, g_attn_row, bf(jnp.pad(w, ((0, 0), (0, d_pad - d_in)))), SAMPLE_PROJ_TILE)
    parts, off = [], 0
    for n in _SPLITS[:-1]:
        parts.append(proj[:, off:off + n])
        off += n
    s_fq, s_fk, s_fv, s_ff, s_gq, s_gk, s_gv, s_ga, s_gr = parts
    lf_new = logsig_bias(s_ff, row(b_fox_f[0]))

    pool_T = lambda c: c.transpose(0, 2, 3, 1).reshape(c.shape[0], W_FOX, c.shape[1])
    decode_args = (_lane_rep(s_fq), _lane_rep(s_fk), _lane_rep(s_fv), _lane_rep(lf_new),
                   pool_T(cache_fox_k[0]), pool_T(cache_fox_v[0]), cache_fox_logf[0].transpose(0, 2, 1))

    xn, qT, ka, kT, vT, vT16, lfT, cT = fox_proj(
        x_prompt, g_attn_row, bfT(w_fq), bf(w_fk), bfT(w_fv), bfT(w_ff), col(b_fox_f[0]), PROMPT_TILE)
    fox_o = fox_attn(qT, cT, ka, vT16, PROMPT_TILE)
    gla_o, p_state = gla(xn, *ga_args, GLA_TILE)
    mkT, mvT = mem_kv(mem_prompt, row(g_mem[0]), bfT(w_mem_k[0]), bfT(w_mem_v[0]))
    mem_o = mem_attn(xn, bf(w_mq), mkT, mvT, PROMPT_TILE)
    N = B * S
    n_steps = N // ROW_TILE
    half = DB // 2
    n_pages = page_table.shape[1]
    plan_a = _plan_decode(0, half, n_steps, n_pages, DECODE_PAGES)
    plan_b = _plan_decode(half, DB - half, n_steps, n_pages, DECODE_PAGES)
    h, fox_o_a = merge(x_prompt.reshape(N, D), xn.reshape(N, D), fox_o.reshape(N, W_FOX),
                       gla_o.reshape(N, W_GLA_V), mem_o.reshape(N, W_MEM), *merge_w, ROW_TILE,
                       decode=(plan_a, page_table, decode_args))
    y_prompt, fox_o_b = mlp(h, *mlp_w, ROW_TILE, decode=(plan_b, page_table, decode_args))
    y_prompt = y_prompt.reshape(B, S, D)
    fox_o_s = jnp.concatenate([fox_o_a, fox_o_b], axis=0)

    heads_out = lambda aT, nh, hd: aT.reshape(aT.shape[0], nh, hd, aT.shape[2]).transpose(0, 3, 1, 2)[None]
    p_fox_k = heads_out(kT, H_FOX, HD_FOX)
    p_fox_v = heads_out(vT, H_FOX, HD_FOX)
    p_fox_logf = lfT.transpose(0, 2, 1)[None]
    p_mem_k = heads_out(mkT, H_MEM, HD_MEM)
    p_mem_v = heads_out(mvT, H_MEM, HD_MEM)

    gla_o_s, s_state = sample_gla(_lane_rep(s_gq), _lane_rep(s_gk), _lane_rep(s_ga), s_gv[:, None, :],
                                  s_gr[:, None, :], state_gla[0], bfT(w_gla_gate_up[0]), col(b_gla_gate[0]),
                                  row(g_gla_norm[0]))
    mem_T = lambda c: c.transpose(0, 2, 3, 1).reshape(c.shape[0], W_MEM, c.shape[1])
    xn_rows = jnp.broadcast_to(xn_s[:, None, :], (DB, SAMPLE_MEM_ROWS, D))
    mem_o_s = mem_attn(xn_rows, bf(w_mq), mem_T(cache_mem_k[0]), mem_T(cache_mem_v[0]), SAMPLE_MEM_ROWS)[:, 0]
    h_s, _ = merge(xs, xn_s, bf(fox_o_s.reshape(DB, W_FOX)), bf(gla_o_s.reshape(DB, W_GLA_V)), mem_o_s,
                   *merge_w, DB)
    y_sample, _ = mlp(h_s, *mlp_w, DB)
    y_sample = y_sample.reshape(DB, 1, D)

    s_fox_k = s_fk.reshape(1, DB, 1, H_FOX, HD_FOX)
    s_fox_v = s_fv.reshape(1, DB, 1, H_FOX, HD_FOX)
    s_fox_logf = lf_new.reshape(1, DB, 1, H_FOX)
    return (y_prompt, y_sample, p_fox_k, p_fox_v, p_fox_logf, p_state[None], p_mem_k, p_mem_v,
            s_fox_k, s_fox_v, s_fox_logf, s_state[None])
```

```python
import functools
from typing import NamedTuple

import jax
import jax.numpy as jnp
from jax import lax
from jax.experimental import pallas as pl
from jax.experimental.pallas import tpu as pltpu

F32 = jnp.float32
BF16 = jnp.bfloat16

H_FOX = 16
HD_FOX = 64
W_FOX = H_FOX * HD_FOX
H_GLA = 4
DK_GLA = 128
DV_GLA = 256
W_GLA_K = H_GLA * DK_GLA
W_GLA_V = H_GLA * DV_GLA
GLA_RANK = 16
GLA_NORMALIZER = 16.0
GLA_CHUNK = 64
MEM_LEN = 256
H_MEM = 4
HD_MEM = 64
W_MEM = H_MEM * HD_MEM
N_BRANCH = 3
EPS = 1e-6

LANES = 128
FOX_AUG = 128
VMEM_LIMIT = 56 * 1024 * 1024


def _dot(a, b):
    return jnp.dot(a, b, preferred_element_type=F32)


def _dot_nt(a, b):
    return lax.dot_general(a, b, (((1,), (1,)), ((), ())), preferred_element_type=F32)


def _split_bf16(x, parts):
    out = []
    r = x
    for _ in range(parts):
        p = r.astype(BF16)
        out.append(p)
        r = r - p.astype(F32)
    return out


def _log_sigmoid(x):
    return jnp.minimum(x, 0.0) - jnp.log1p(jnp.exp(-jnp.abs(x)))


def _sigmoid(x):
    return 1.0 / (1.0 + jnp.exp(-x))


def _rmsnorm(x, g):
    ms = jnp.mean(x * x, axis=-1, keepdims=True)
    return x * lax.rsqrt(ms + EPS) * g


def _lane_cumsum(x, reverse=False):
    rows, n = x.shape
    lane = lax.broadcasted_iota(jnp.int32, (rows, LANES), 1)
    blocks = []
    for i in range(n // LANES):
        y = x[:, i * LANES:(i + 1) * LANES]
        k = 1
        while k < LANES:
            if reverse:
                y = y + jnp.where(lane < LANES - k, pltpu.roll(y, LANES - k, 1), 0.0)
            else:
                y = y + jnp.where(lane >= k, pltpu.roll(y, k, 1), 0.0)
            k *= 2
        blocks.append(y)
    order = range(len(blocks) - 1, -1, -1) if reverse else range(len(blocks))
    edge = 0 if reverse else LANES - 1
    carry = None
    for i in order:
        if carry is not None:
            blocks[i] = blocks[i] + carry
        carry = blocks[i][:, edge:edge + 1]
    return blocks[0] if len(blocks) == 1 else jnp.concatenate(blocks, axis=1)


def _const_spec(shape):
    return pl.BlockSpec(shape, lambda *_: (0,) * len(shape), pipeline_mode=pl.Buffered(1))


def _params(sem):
    return pltpu.CompilerParams(dimension_semantics=sem, vmem_limit_bytes=VMEM_LIMIT)


def _split3_f32(x):
    hi = x.astype(BF16).astype(F32)
    r = x - hi
    mid = r.astype(BF16).astype(F32)
    lo = (r - mid).astype(BF16).astype(F32)
    return hi, mid, lo


def _fox_proj_kernel(x_ref, g_ref, wqT_ref, wk_ref, wvT_ref, wffT_ref, bcol_ref, e_ref,
                     xn_ref, qT_ref, ka_ref, kT_ref, vT_ref, vT16_ref, lfT_ref, cT_ref, carry_ref):
    i = pl.program_id(1)
    ts = x_ref.shape[1]

    @pl.when(i == 0)
    def _():
        carry_ref[...] = jnp.zeros_like(carry_ref)

    xn = _rmsnorm(x_ref[0], g_ref[...]).astype(BF16)
    xn_ref[0] = xn
    qT_ref[0] = (_dot_nt(wqT_ref[...], xn) * (HD_FOX ** -0.5)).astype(BF16)
    vT = _dot_nt(wvT_ref[...], xn)
    vT_ref[0] = vT
    vT16_ref[0] = vT.astype(BF16)
    k = _dot(xn, wk_ref[...])
    kT_ref[0] = k.T

    lfT = _log_sigmoid(_dot_nt(wffT_ref[...], xn) + bcol_ref[...])
    lfT_ref[0] = lfT
    cT = _lane_cumsum(lfT) + carry_ref[...]
    carry_ref[...] = cT[:, ts - 1:ts]
    cT_ref[0] = cT

    hi, mid, lo = _split3_f32(-cT)
    stack = jnp.concatenate([hi, mid, lo, jnp.ones((H_FOX, ts), F32),
                             jnp.zeros((LANES - 4 * H_FOX, ts), F32)], axis=0)
    ex = _dot(stack.T.astype(BF16), e_ref[...])
    lower = lax.broadcasted_iota(jnp.int32, (ts, LANES), 1) < HD_FOX
    for g in range(H_FOX // 2):
        kg = k[:, g * LANES:(g + 1) * LANES]
        eg = ex[:, g * LANES:(g + 1) * LANES]
        ka_ref[0, :, (2 * g) * FOX_AUG:(2 * g + 1) * FOX_AUG] = jnp.where(lower, kg, eg).astype(BF16)
        ka_ref[0, :, (2 * g + 1) * FOX_AUG:(2 * g + 2) * FOX_AUG] = jnp.where(lower, eg, kg).astype(BF16)


def _fox_bias_selector():
    e = jnp.zeros((LANES, W_FOX), F32)
    for h in range(H_FOX):
        base = (h // 2) * LANES + (HD_FOX if h % 2 == 0 else 0)
        for j in range(3):
            e = e.at[3 * H_FOX, base + j].set(1.0)
            e = e.at[j * H_FOX + h, base + 3 + j].set(1.0)
    return e.astype(BF16)


def fox_proj(x, g, wqT, wk, wvT, wffT, bcol, ts):
    B, S, D = x.shape
    e = _fox_bias_selector()
    out_shape = (
        jax.ShapeDtypeStruct((B, S, D), BF16),
        jax.ShapeDtypeStruct((B, W_FOX, S), BF16),
        jax.ShapeDtypeStruct((B, S, H_FOX * FOX_AUG), BF16),
        jax.ShapeDtypeStruct((B, W_FOX, S), F32),
        jax.ShapeDtypeStruct((B, W_FOX, S), F32),
        jax.ShapeDtypeStruct((B, W_FOX, S), BF16),
        jax.ShapeDtypeStruct((B, H_FOX, S), F32),
        jax.ShapeDtypeStruct((B, H_FOX, S), F32),
    )
    tile = lambda n: pl.BlockSpec((1, ts, n), lambda b, i: (b, i, 0))
    tileT = lambda n: pl.BlockSpec((1, n, ts), lambda b, i: (b, 0, i))
    consts = (g, wqT, wk, wvT, wffT, bcol, e)
    return pl.pallas_call(
        _fox_proj_kernel,
        grid=(B, S // ts),
        in_specs=[tile(D)] + [_const_spec(c.shape) for c in consts],
        out_specs=(tile(D), tileT(W_FOX), tile(H_FOX * FOX_AUG), tileT(W_FOX), tileT(W_FOX),
                   tileT(W_FOX), tileT(H_FOX), tileT(H_FOX)),
        out_shape=out_shape,
        scratch_shapes=[pltpu.VMEM((H_FOX, 1), F32)],
        compiler_params=_params(("parallel", "arbitrary")),
        name="fox_proj",
    )(x, *consts)


L_ROWS = 16
FOX_HEAD_GROUP = 4


def _fox_attn_kernel(qi_ref, kj_ref, qT_ref, cT_ref, ka_ref, vT_ref, o_ref,
                     qa_ref, s_ref, mx_ref, m_ref, acc_ref, *, tq):
    group = pl.program_id(1)
    n_heads = qa_ref.shape[0]
    S = qT_ref.shape[2]
    tk = tq
    n_pairs = qi_ref.shape[0]

    row8 = lax.broadcasted_iota(jnp.int32, (8, S), 0)
    for h in range(n_heads):
        hi, mid, lo = _split3_f32(cT_ref[0, pl.ds(n_heads * group + h, 1), :])
        ext8 = jnp.where(row8 == 0, hi, jnp.where(row8 == 1, mid, jnp.where(
            row8 == 2, lo, jnp.where(row8 < 6, 1.0, 0.0))))
        ext = jnp.concatenate([ext8, jnp.zeros((HD_FOX - 8, S), F32)], axis=0).astype(BF16)
        qh = qT_ref[0, h * HD_FOX:(h + 1) * HD_FOX, :]
        qa_ref[h] = jnp.concatenate([qh, ext] if h % 2 == 0 else [ext, qh], axis=0)
        m_ref[h] = jnp.full((1, tq), -jnp.inf, F32)
        acc_ref[h] = jnp.zeros((HD_FOX + L_ROWS, tq), F32)

    krow = lax.broadcasted_iota(jnp.int32, (tk, tq), 0)
    qcol = lax.broadcasted_iota(jnp.int32, (tk, tq), 1)
    causal = krow <= qcol
    ones = jnp.ones((L_ROWS, tk), BF16)

    def scores(t, slot, h):
        qoff = pl.multiple_of(qi_ref[t] * tq, tq)
        koff = pl.multiple_of(kj_ref[t] * tk, tk)
        sT = _dot(ka_ref[0, pl.ds(koff, tk), h * FOX_AUG:(h + 1) * FOX_AUG],
                  qa_ref[h, :, pl.ds(qoff, tq)])
        s_ref[slot, h] = sT
        mx_ref[slot, h] = jnp.max(sT, axis=0, keepdims=True)

    def consume(t, slot, h, diagonal):
        koff = pl.multiple_of(kj_ref[t] * tk, tk)
        sT = s_ref[slot, h]
        if diagonal:
            sT = jnp.where(causal, sT, -jnp.inf)
            mx = jnp.max(sT, axis=0, keepdims=True)
        else:
            mx = mx_ref[slot, h]
        m_prev = m_ref[h]
        m_next = jnp.maximum(m_prev, mx)
        pT = jnp.exp(sT - m_next).astype(BF16)
        alpha = jnp.exp(m_prev - m_next)
        va = jnp.concatenate([vT_ref[0, h * HD_FOX:(h + 1) * HD_FOX, pl.ds(koff, tk)], ones], axis=0)
        acc = alpha * acc_ref[h] + _dot(va, pT)
        if not diagonal:
            m_ref[h] = m_next
            acc_ref[h] = acc
            return None
        m_ref[h] = jnp.full((1, tq), -jnp.inf, F32)
        acc_ref[h] = jnp.zeros((HD_FOX + L_ROWS, tq), F32)
        return acc[:HD_FOX] / acc[HD_FOX:HD_FOX + 1]

    def trip(t, slot, diagonal):
        nxt = jnp.minimum(t + 1, n_pairs - 1)
        outs = []
        for h in range(n_heads):
            scores(nxt, 1 - slot, h)
            outs.append(consume(t, slot, h, diagonal))
        if diagonal:
            qoff = pl.multiple_of(qi_ref[t] * tq, tq)
            o_ref[0, pl.ds(qoff, tq), :] = jnp.concatenate(outs, axis=0).T.astype(o_ref.dtype)

    def either(t, slot):
        is_diag = kj_ref[t] == qi_ref[t]

        @pl.when(is_diag)
        def _():
            trip(t, slot, True)

        @pl.when(jnp.logical_not(is_diag))
        def _():
            trip(t, slot, False)

    for h in range(n_heads):
        scores(0, 0, h)

    def body(u, carry):
        either(2 * u, 0)
        either(2 * u + 1, 1)
        return carry

    lax.fori_loop(0, n_pairs // 2, body, 0)
    if n_pairs % 2:
        either(n_pairs - 1, 0)


def fox_attn(qT, cT, ka, vT16, tq):
    B, _, S = qT.shape
    nq = S // tq
    qi = jnp.asarray([i for i in range(nq) for _ in range(i + 1)], jnp.int32)
    kj = jnp.asarray([j for i in range(nq) for j in range(i + 1)], jnp.int32)
    hg = FOX_HEAD_GROUP
    grid_spec = pltpu.PrefetchScalarGridSpec(
        num_scalar_prefetch=2,
        grid=(B, H_FOX // hg),
        in_specs=[pl.BlockSpec((1, hg * HD_FOX, S), lambda b, p, *_: (b, p, 0)),
                  pl.BlockSpec((1, H_FOX, S), lambda b, p, *_: (b, 0, 0)),
                  pl.BlockSpec((1, S, hg * FOX_AUG), lambda b, p, *_: (b, 0, p)),
                  pl.BlockSpec((1, hg * HD_FOX, S), lambda b, p, *_: (b, p, 0))],
        out_specs=pl.BlockSpec((1, S, hg * HD_FOX), lambda b, p, *_: (b, 0, p)),
        scratch_shapes=[pltpu.VMEM((hg, FOX_AUG, S), BF16),
                        pltpu.VMEM((2, hg, tq, tq), F32),
                        pltpu.VMEM((2, hg, 1, tq), F32),
                        pltpu.VMEM((hg, 1, tq), F32),
                        pltpu.VMEM((hg, HD_FOX + L_ROWS, tq), F32)],
    )
    return pl.pallas_call(
        functools.partial(_fox_attn_kernel, tq=tq),
        grid_spec=grid_spec,
        out_shape=jax.ShapeDtypeStruct((B, S, W_FOX), BF16),
        compiler_params=_params(("parallel", "parallel")),
        name="fox_attn",
    )(qi, kj, qT, cT, ka, vT16)


def _gla_kernel(xn_ref, wqT_ref, wkT_ref, wv_ref, waT_ref, wr_ref, wuT_ref, bcol_ref, g_ref,
                o_ref, st_ref, s_ref, oacc_ref):
    t = pl.program_id(1)
    T = xn_ref.shape[1]
    C = GLA_CHUNK

    @pl.when(t == 0)
    def _():
        s_ref[...] = jnp.zeros_like(s_ref)

    xn = xn_ref[0]
    qT = _dot_nt(wqT_ref[...], xn) * (DK_GLA ** -0.5)
    kT = _dot_nt(wkT_ref[...], xn)
    v = _dot(xn, wv_ref[...]).astype(BF16)
    gaT = _dot_nt(waT_ref[...], xn).astype(BF16)
    laT = _log_sigmoid(_dot(wuT_ref[...], gaT) + bcol_ref[...]) / GLA_NORMALIZER

    r = lax.broadcasted_iota(jnp.int32, (T, T), 0)
    c_ = lax.broadcasted_iota(jnp.int32, (T, T), 1)
    same = (r // C) == (c_ // C)
    upper = jnp.logical_and(same, r <= c_).astype(BF16)
    block = same.astype(BF16)
    bT = totT = None
    for part in _split_bf16(laT, 2):
        d = _dot(part, upper)
        bT = d if bT is None else bT + d
        d = _dot(part, block)
        totT = d if totT is None else totT + d

    qe = (qT * jnp.exp(bT)).T.astype(BF16)
    kdT = (kT * jnp.exp(-bT)).astype(BF16)
    keT = (kT * jnp.exp(totT - bT)).astype(BF16)
    decT = jnp.exp(totT)
    in_chunk_causal = jnp.logical_and(same, c_ <= r)

    n_chunks = T // C
    for h in range(H_GLA):
        kk = slice(h * DK_GLA, (h + 1) * DK_GLA)
        vv = slice(h * DV_GLA, (h + 1) * DV_GLA)
        a = jnp.where(in_chunk_causal, _dot(qe[:, kk], kdT[kk, :]), 0.0)
        o_intra = _dot(a.astype(BF16), v[:, vv])
        ds = [_dot(keT[kk, c * C:(c + 1) * C], v[c * C:(c + 1) * C, vv]) for c in range(n_chunks)]
        s = s_ref[h]
        for c in range(n_chunks):
            rows = slice(c * C, (c + 1) * C)
            oacc_ref[rows, vv] = o_intra[rows] + _dot(qe[rows, kk], s.astype(BF16))
            s = decT[kk, c * C:c * C + 1] * s + ds[c]
        s_ref[h] = s

    gr = _dot(xn, wr_ref[...])
    gate = gr * _sigmoid(gr)
    for h in range(H_GLA):
        vv = slice(h * DV_GLA, (h + 1) * DV_GLA)
        y = _rmsnorm(oacc_ref[:, vv], g_ref[...])
        o_ref[0, :, vv] = (y * gate[:, vv]).astype(o_ref.dtype)

    @pl.when(t == pl.num_programs(1) - 1)
    def _():
        st_ref[0] = s_ref[...]


def gla(xn, wqT, wkT, wv, waT, wr, wuT, bcol, g, T):
    B, S, D = xn.shape
    consts = (wqT, wkT, wv, waT, wr, wuT, bcol, g)
    return pl.pallas_call(
        _gla_kernel,
        grid=(B, S // T),
        in_specs=[pl.BlockSpec((1, T, D), lambda b, t: (b, t, 0))] + [_const_spec(w.shape) for w in consts],
        out_specs=(pl.BlockSpec((1, T, W_GLA_V), lambda b, t: (b, t, 0)),
                   pl.BlockSpec((1, H_GLA, DK_GLA, DV_GLA), lambda b, t: (b, 0, 0, 0))),
        out_shape=(jax.ShapeDtypeStruct((B, S, W_GLA_V), BF16),
                   jax.ShapeDtypeStruct((B, H_GLA, DK_GLA, DV_GLA), F32)),
        scratch_shapes=[pltpu.VMEM((H_GLA, DK_GLA, DV_GLA), F32), pltpu.VMEM((T, W_GLA_V), F32)],
        compiler_params=_params(("parallel", "arbitrary")),
        name="gla",
    )(xn, *consts)


def _mem_kv_kernel(m_ref, g_ref, wkT_ref, wvT_ref, kT_ref, vT_ref):
    mn = _rmsnorm(m_ref[0], g_ref[...]).astype(BF16)
    kT_ref[0] = _dot_nt(wkT_ref[...], mn)
    vT_ref[0] = _dot_nt(wvT_ref[...], mn)


def mem_kv(mem, g, wkT, wvT):
    B, M, D = mem.shape
    blk = pl.BlockSpec((1, W_MEM, M), lambda b: (b, 0, 0))
    return pl.pallas_call(
        _mem_kv_kernel,
        grid=(B,),
        in_specs=[pl.BlockSpec((1, M, D), lambda b: (b, 0, 0)), _const_spec(g.shape),
                  _const_spec(wkT.shape), _const_spec(wvT.shape)],
        out_specs=(blk, blk),
        out_shape=(jax.ShapeDtypeStruct((B, W_MEM, M), F32),) * 2,
        compiler_params=_params(("parallel",)),
        name="mem_kv",
    )(mem, g, wkT, wvT)


def _mem_attn_kernel(xn_ref, wq_ref, kT_ref, vT_ref, o_ref):
    q = _dot(xn_ref[0], wq_ref[...]).astype(BF16)
    outs = []
    for h in range(H_MEM):
        hh = slice(h * HD_MEM, (h + 1) * HD_MEM)
        s = _dot(q[:, hh], kT_ref[0, hh, :].astype(BF16)) * (HD_MEM ** -0.5)
        p = jnp.exp(s - jnp.max(s, axis=1, keepdims=True))
        p = p / jnp.sum(p, axis=1, keepdims=True)
        outs.append(_dot_nt(p.astype(BF16), vT_ref[0, hh, :].astype(BF16)))
    o_ref[0] = jnp.concatenate(outs, axis=1).astype(o_ref.dtype)


def mem_attn(xn, wq, kT, vT, tl):
    B, L, D = xn.shape
    M = kT.shape[2]
    return pl.pallas_call(
        _mem_attn_kernel,
        grid=(B, L // tl),
        in_specs=[pl.BlockSpec((1, tl, D), lambda b, i: (b, i, 0)), _const_spec(wq.shape),
                  pl.BlockSpec((1, W_MEM, M), lambda b, i: (b, 0, 0)),
                  pl.BlockSpec((1, W_MEM, M), lambda b, i: (b, 0, 0))],
        out_specs=pl.BlockSpec((1, tl, W_MEM), lambda b, i: (b, i, 0)),
        out_shape=jax.ShapeDtypeStruct((B, L, W_MEM), BF16),
        compiler_params=_params(("parallel", "parallel")),
        name="mem_attn",
    )(xn, wq, kT, vT)


HOST_CHUNKS = 4


def _run_host(n_own_in, n_own_out, chunk_fn, finish_fn, refs, plan):
    if plan is None:
        for c in range(HOST_CHUNKS):
            chunk_fn(c)
        finish_fn()
        return
    assert plan.groups == HOST_CHUNKS
    pt_ref, refs = refs[0], refs[1:]
    dec_in = refs[n_own_in:n_own_in + 7]
    dec_out = refs[n_own_in + 7 + n_own_out]
    dec_scratch = refs[n_own_in + 7 + n_own_out + 1:]
    _decode_side_task(plan, pl.program_id(0), pl.num_programs(0),
                      (pt_ref, *dec_in, dec_out, *dec_scratch), chunk_fn)
    finish_fn()


def _host_call(kernel, name, n_rows, tm, own_in_specs, own_out_spec, own_out_shape, own_args, decode):
    if decode is None:
        return pl.pallas_call(
            functools.partial(kernel, plan=None), grid=(n_rows // tm,), in_specs=own_in_specs,
            out_specs=own_out_spec, out_shape=own_out_shape, compiler_params=_params(("parallel",)),
            name=name)(*own_args), None
    plan, page_table, dec_args = decode
    dec_in_specs, dec_out_spec, dec_out_shape, dec_scratch = _decode_host_specs(plan)
    grid_spec = pltpu.PrefetchScalarGridSpec(
        num_scalar_prefetch=1, grid=(n_rows // tm,),
        in_specs=list(own_in_specs) + dec_in_specs,
        out_specs=(own_out_spec, dec_out_spec),
        scratch_shapes=dec_scratch)
    return pl.pallas_call(
        functools.partial(kernel, plan=plan), grid_spec=grid_spec,
        out_shape=(own_out_shape, dec_out_shape),
        compiler_params=_params(("arbitrary",)), name=name + "_decode")(page_table, *own_args, *dec_args)


def _merge_kernel(*refs, plan):
    own = refs[1:] if plan is not None else refs
    x_ref, xn_ref, fo_ref, go_ref, mo_ref, wgl_ref, wuf_ref, wug_ref, wum_ref, wout_ref = own[:10]
    h_ref = own[10 + (7 if plan is not None else 0)]
    D = x_ref.shape[1]
    branches = ((fo_ref, wuf_ref), (go_ref, wug_ref), (mo_ref, wum_ref))
    merged = []

    def chunk(c):
        if c < N_BRANCH:
            br, w = branches[c]
            gate = _sigmoid(_dot(xn_ref[...], wgl_ref[:, c * D:(c + 1) * D]))
            merged.append(gate * _dot(br[...], w[...]))
        else:
            m = (merged[0] + merged[1] + merged[2]).astype(BF16)
            h_ref[...] = x_ref[...] + _dot(m, wout_ref[...])

    _run_host(10, 1, chunk, lambda: None, refs, plan)


def merge(x, xn, fo, go, mo, wgl, wuf, wug, wum, wout, tm, decode=None):
    N, D = x.shape
    row = lambda n: pl.BlockSpec((tm, n), lambda i, *_: (i, 0))
    consts = (wgl, wuf, wug, wum, wout)
    in_specs = [row(D), row(D), row(fo.shape[1]), row(go.shape[1]), row(mo.shape[1])] \
        + [_const_spec(w.shape) for w in consts]
    return _host_call(_merge_kernel, "merge", N, tm, in_specs, row(D), jax.ShapeDtypeStruct((N, D), F32),
                      (x, xn, fo, go, mo, *consts), decode)


def _mlp_kernel(*refs, plan):
    own = refs[1:] if plan is not None else refs
    h_ref, g_ref, w1_ref, w2_ref, gf_ref = own[:5]
    y_ref = own[5 + (7 if plan is not None else 0)]
    fc = w1_ref.shape[1] // HOST_CHUNKS
    state = {}

    def chunk(c):
        if c == 0:
            state["acc"] = h_ref[...]
            state["hn"] = _rmsnorm(state["acc"], g_ref[...]).astype(BF16)
        cc = slice(c * fc, (c + 1) * fc)
        u = jnp.maximum(_dot(state["hn"], w1_ref[:, cc]), 0.0)
        state["acc"] = state["acc"] + _dot((u * u).astype(BF16), w2_ref[cc, :])

    def finish():
        y_ref[...] = _rmsnorm(state["acc"], gf_ref[...])

    _run_host(5, 1, chunk, finish, refs, plan)


def mlp(h, g, w1, w2, gf, tm, decode=None):
    N, D = h.shape
    row = pl.BlockSpec((tm, D), lambda i, *_: (i, 0))
    in_specs = [row, _const_spec(g.shape), _const_spec(w1.shape), _const_spec(w2.shape), _const_spec(gf.shape)]
    return _host_call(_mlp_kernel, "mlp", N, tm, in_specs, row, jax.ShapeDtypeStruct((N, D), F32),
                      (h, g, w1, w2, gf), decode)


def _sample_proj_kernel(x_ref, g_ref, w_ref, xn_ref, o_ref):
    xn = _rmsnorm(x_ref[...], g_ref[...]).astype(BF16)
    xn_ref[...] = xn
    o_ref[...] = _dot(xn, w_ref[...])


def sample_proj(x, g, w, tn):
    N, D = x.shape
    NP = w.shape[1]
    return pl.pallas_call(
        _sample_proj_kernel,
        grid=(NP // tn,),
        in_specs=[_const_spec((N, D)), _const_spec(g.shape), pl.BlockSpec((D, tn), lambda j: (0, j))],
        out_specs=(pl.BlockSpec((N, D), lambda j: (0, 0)), pl.BlockSpec((N, tn), lambda j: (0, j))),
        out_shape=(jax.ShapeDtypeStruct((N, D), BF16), jax.ShapeDtypeStruct((N, NP), F32)),
        compiler_params=_params(("arbitrary",)),
        name="sample_proj",
    )(x, g, w)


def _logsig_kernel(ff_ref, b_ref, o_ref):
    o_ref[...] = _log_sigmoid(ff_ref[...] + b_ref[...])


def logsig_bias(ff, b):
    return pl.pallas_call(_logsig_kernel, out_shape=jax.ShapeDtypeStruct(ff.shape, F32), name="logsig")(ff, b)


RING_AHEAD = 3


class _DecodePlan(NamedTuple):
    first_b: int
    n_b: int
    steps_per_b: int
    groups: int
    pages: int
    n_pages: int


def _plan_decode(first_b, n_b, n_steps, n_pages, pages):
    steps_per_b, rem = divmod(n_steps, n_b)
    assert rem == 0 and steps_per_b >= 1, (n_steps, n_b)
    per_step, rem = divmod(n_pages, steps_per_b)
    assert rem == 0, (n_pages, steps_per_b)
    groups, rem = divmod(per_step, pages)
    assert rem == 0 and groups > RING_AHEAD, (per_step, pages)
    return _DecodePlan(first_b, n_b, steps_per_b, groups, pages, n_pages)


def _decode_side_task(plan, i, n_steps, refs, host_chunk):
    (pt_ref, q_ref, kn_ref, vn_ref, lfn_ref, k_hbm, v_hbm, lf_hbm, o_ref,
     kbuf, vbuf, lfbuf, sem, qs_ref, m_ref, l_ref, r_ref, acc_ref) = refs
    G = plan.pages
    part = i % plan.steps_per_b
    lane = lax.broadcasted_iota(jnp.int32, (H_FOX, LANES), 1)
    lane_w = lax.broadcasted_iota(jnp.int32, (W_FOX, LANES), 1)

    def group_copies(step, g, slot):
        b = plan.first_b + step // plan.steps_per_b
        done = (step % plan.steps_per_b) * plan.groups * G + g * G
        copies = []
        for k in range(G):
            page = pt_ref[b, plan.n_pages - 1 - (done + k)]
            copies += [pltpu.make_async_copy(k_hbm.at[page], kbuf.at[slot, k], sem.at[slot, 0]),
                       pltpu.make_async_copy(v_hbm.at[page], vbuf.at[slot, k], sem.at[slot, 1]),
                       pltpu.make_async_copy(lf_hbm.at[page], lfbuf.at[slot, k], sem.at[slot, 2])]
        return copies

    def head_scores(k_ref):
        return jnp.concatenate(
            [jnp.sum(k_ref[h * HD_FOX:(h + 1) * HD_FOX, :] * qs_ref[h * HD_FOX:(h + 1) * HD_FOX, :],
                     axis=0, keepdims=True) for h in range(H_FOX)], axis=0)

    def start_group(target):
        step_off, g_t = divmod(target, plan.groups)
        if step_off == 0:
            for c in group_copies(i, g_t, g_t):
                c.start()
        else:
            @pl.when(i + step_off < n_steps)
            def _():
                for c in group_copies(i + step_off, g_t, g_t):
                    c.start()

    @pl.when(i == 0)
    def _():
        for g0 in range(RING_AHEAD):
            for c in group_copies(0, g0, g0):
                c.start()

    @pl.when(part == 0)
    def _():
        qs_ref[...] = q_ref[0] * (HD_FOX ** -0.5)
        m_ref[...] = head_scores(kn_ref.at[0])
        l_ref[...] = jnp.where(lane == 0, 1.0, 0.0)
        r_ref[...] = jnp.zeros_like(r_ref)
        acc_ref[...] = jnp.where(lane_w == 0, vn_ref[0], 0.0)

    for g in range(plan.groups):
        slot = g
        start_group(g + RING_AHEAD)
        host_chunk(g)
        for c in group_copies(i, g, slot):
            c.wait()

        r_run = r_ref[...]
        scores = []
        for k in range(G):
            lf = lfbuf[slot, k]
            incl = _lane_cumsum(lf, reverse=True)
            scores.append(head_scores(kbuf.at[slot, k]) + lfn_ref[0] + (incl - lf + r_run))
            r_run = r_run + incl[:, 0:1]
        r_ref[...] = r_run
        m_prev = m_ref[...]
        m_new = m_prev
        for s in scores:
            m_new = jnp.maximum(m_new, jnp.max(s, axis=1, keepdims=True))
        alpha = jnp.exp(m_prev - m_new)
        probs = [jnp.exp(s - m_new) for s in scores]
        l_ref[...] = alpha * l_ref[...] + functools.reduce(lambda a, b: a + b, probs)
        m_ref[...] = m_new
        for h in range(H_FOX):
            rows = slice(h * HD_FOX, (h + 1) * HD_FOX)
            a = acc_ref[rows, :] * alpha[h:h + 1, :]
            for k in range(G):
                a = a + vbuf[slot, k, rows, :] * probs[k][h:h + 1, :]
            acc_ref[rows, :] = a

    @pl.when(part == plan.steps_per_b - 1)
    def _():
        inv = 1.0 / jnp.sum(l_ref[...], axis=1, keepdims=True)
        for h in range(H_FOX):
            rows = slice(h * HD_FOX, (h + 1) * HD_FOX)
            acc_ref[rows, :] = acc_ref[rows, :] * inv[h:h + 1, :]
        ones = jnp.ones((8, LANES), BF16)
        out = None
        for piece in _split_bf16(acc_ref[...], 3):
            t = _dot_nt(ones, piece)
            out = t if out is None else out + t
        o_ref[0] = out[0:1, :]


def _decode_host_specs(plan):
    b_of = lambda i: plan.first_b + i // plan.steps_per_b
    per_b = lambda n: pl.BlockSpec((1, n, LANES), lambda i, pt: (b_of(i), 0, 0))
    pool = pl.BlockSpec(memory_space=pl.ANY)
    in_specs = [per_b(W_FOX), per_b(W_FOX), per_b(W_FOX), per_b(H_FOX), pool, pool, pool]
    out_spec = pl.BlockSpec((1, 1, W_FOX), lambda i, pt: (i // plan.steps_per_b, 0, 0))
    out_shape = jax.ShapeDtypeStruct((plan.n_b, 1, W_FOX), F32)
    G, slots = plan.pages, plan.groups
    scratch = [pltpu.VMEM((slots, G, W_FOX, LANES), F32), pltpu.VMEM((slots, G, W_FOX, LANES), F32),
               pltpu.VMEM((slots, G, H_FOX, LANES), F32), pltpu.SemaphoreType.DMA((slots, 3)),
               pltpu.VMEM((W_FOX, LANES), F32), pltpu.VMEM((H_FOX, LANES), F32),
               pltpu.VMEM((H_FOX, LANES), F32), pltpu.VMEM((H_FOX, LANES), F32),
               pltpu.VMEM((W_FOX, LANES), F32)]
    return in_specs, out_spec, out_shape, scratch


def _sample_gla_kernel(q_ref, k_ref, ga_ref, v_ref, gr_ref, s0_ref, wuT_ref, bcol_ref, g_ref, o_ref, st_ref):
    z = _dot(wuT_ref[...], ga_ref[0].astype(BF16)) + bcol_ref[...]
    la = _log_sigmoid(z) / GLA_NORMALIZER
    ea = jnp.exp(la)
    k = k_ref[0]
    qe = q_ref[0] * (DK_GLA ** -0.5) * ea
    kd = k * jnp.exp(-la)
    gr = gr_ref[0]
    gate = gr * _sigmoid(gr)
    wide = lambda a: jnp.concatenate([a, a], axis=1)
    for h in range(H_GLA):
        kk = slice(h * DK_GLA, (h + 1) * DK_GLA)
        vv = slice(h * DV_GLA, (h + 1) * DV_GLA)
        s_old = s0_ref[0, h]
        v_h = v_ref[0][:, vv]
        a = jnp.sum(qe[kk] * kd[kk], axis=0, keepdims=True)
        o = wide(a) * v_h + jnp.sum(wide(qe[kk]) * s_old, axis=0, keepdims=True)
        st_ref[0, h] = wide(ea[kk]) * s_old + wide(k[kk]) * v_h
        o_ref[0, :, vv] = _rmsnorm(o, g_ref[...]) * gate[:, vv]


def sample_gla(q_rep, k_rep, ga_rep, v, gr, s0, wuT, bcol, g):
    DB = q_rep.shape[0]
    per_b = lambda a: pl.BlockSpec((1,) + a.shape[1:], lambda b: (b,) + (0,) * (a.ndim - 1))
    return pl.pallas_call(
        _sample_gla_kernel,
        grid=(DB,),
        in_specs=[per_b(q_rep), per_b(k_rep), per_b(ga_rep), per_b(v), per_b(gr), per_b(s0),
                  _const_spec(wuT.shape), _const_spec(bcol.shape), _const_spec(g.shape)],
        out_specs=(per_b(v), per_b(s0)),
        out_shape=(jax.ShapeDtypeStruct(v.shape, F32), jax.ShapeDtypeStruct(s0.shape, F32)),
        compiler_params=_params(("parallel",)),
        name="sample_gla",
    )(q_rep, k_rep, ga_rep, v, gr, s0, wuT, bcol, g)


_SPLITS = (W_FOX, W_FOX, W_FOX, H_FOX, W_GLA_K, W_GLA_K, W_GLA_V, GLA_RANK, W_GLA_V, W_MEM)
PROMPT_TILE = 512
GLA_TILE = 256
ROW_TILE = 256
DECODE_PAGES = 4
SAMPLE_PROJ_TILE = 512
SAMPLE_MEM_ROWS = 8


def _lane_rep(a):
    return jnp.broadcast_to(a[..., None], a.shape + (LANES,))


def kernel(x_prompt, x_sample, mem_prompt, cache_fox_k, cache_fox_v, cache_fox_logf, state_gla, cache_mem_k, cache_mem_v, page_table, g_attn, w_in, b_fox_f, w_gla_gate_up, b_gla_gate, g_gla_norm, g_mem, w_mem_k, w_mem_v, w_up_fox, w_up_gla, w_up_mem, w_out, g_mlp, w_mlp_in, w_mlp_out, g_final):
    B, S, D = x_prompt.shape
    DB = x_sample.shape[0]
    assert w_in.shape[0] == 1 and x_sample.shape[1] == 1

    w = w_in[0]
    segs, off = [], 0
    for n in _SPLITS:
        segs.append(w[:, off:off + n])
        off += n
    w_fq, w_fk, w_fv, w_ff, w_gq, w_gk, w_gv, w_ga, w_gr, w_mq = segs
    w_gl = w[:, off:]
    bf = lambda a: a.astype(BF16)
    bfT = lambda a: a.T.astype(BF16)
    row = lambda a: a.reshape(1, -1)
    col = lambda a: a.reshape(-1, 1)

    ga_args = (bfT(w_gq), bfT(w_gk), bf(w_gv), bfT(w_ga), bf(w_gr), bfT(w_gla_gate_up[0]),
               col(b_gla_gate[0]), row(g_gla_norm[0]))
    merge_w = (bf(w_gl), bf(w_up_fox[0]), bf(w_up_gla[0]), bf(w_up_mem[0]), bf(w_out[0]))
    mlp_w = (row(g_mlp[0]), bf(w_mlp_in[0]), bf(w_mlp_out[0]), row(g_final))
    g_attn_row = row(g_attn[0])

    xs = x_sample.reshape(DB, D)
    d_in = w.shape[1]
    d_pad = -(-d_in // SAMPLE_PROJ_TILE) * SAMPLE_PROJ_TILE
    xn_s, proj = sample_proj(xs, g_attn_row, bf(jnp.pad(w, ((0, 0), (0, d_pad - d_in)))), SAMPLE_PROJ_TILE)
    parts, off = [], 0
    for n in _SPLITS[:-1]:
        parts.append(proj[:, off:off + n])
        off += n
    s_fq, s_fk, s_fv, s_ff, s_gq, s_gk, s_gv, s_ga, s_gr = parts
    lf_new = logsig_bias(s_ff, row(b_fox_f[0]))

    pool_T = lambda c: c.transpose(0, 2, 3, 1).reshape(c.shape[0], W_FOX, c.shape[1])
    decode_args = (_lane_rep(s_fq), _lane_rep(s_fk), _lane_rep(s_fv), _lane_rep(lf_new),
                   pool_T(cache_fox_k[0]), pool_T(cache_fox_v[0]), cache_fox_logf[0].transpose(0, 2, 1))

    xn, qT, ka, kT, vT, vT16, lfT, cT = fox_proj(
        x_prompt, g_attn_row, bfT(w_fq), bf(w_fk), bfT(w_fv), bfT(w_ff), col(b_fox_f[0]), PROMPT_TILE)
    fox_o = fox_attn(qT, cT, ka, vT16, PROMPT_TILE)
    gla_o, p_state = gla(xn, *ga_args, GLA_TILE)
    mkT, mvT = mem_kv(mem_prompt, row(g_mem[0]), bfT(w_mem_k[0]), bfT(w_mem_v[0]))
    mem_o = mem_attn(xn, bf(w_mq), mkT, mvT, PROMPT_TILE)
    N = B * S
    n_steps = N // ROW_TILE
    half = DB // 2
    n_pages = page_table.shape[1]
    plan_a = _plan_decode(0, half, n_steps, n_pages, DECODE_PAGES)
    plan_b = _plan_decode(half, DB - half, n_steps, n_pages, DECODE_PAGES)
    h, fox_o_a = merge(x_prompt.reshape(N, D), xn.reshape(N, D), fox_o.reshape(N, W_FOX),
                       gla_o.reshape(N, W_GLA_V), mem_o.reshape(N, W_MEM), *merge_w, ROW_TILE,
                       decode=(plan_a, page_table, decode_args))
    y_prompt, fox_o_b = mlp(h, *mlp_w, ROW_TILE, decode=(plan_b, page_table, decode_args))
    y_prompt = y_prompt.reshape(B, S, D)
    fox_o_s = jnp.concatenate([fox_o_a, fox_o_b], axis=0)

    heads_out = lambda aT, nh, hd: aT.reshape(aT.shape[0], nh, hd, aT.shape[2]).transpose(0, 3, 1, 2)[None]
    p_fox_k = heads_out(kT, H_FOX, HD_FOX)
    p_fox_v = heads_out(vT, H_FOX, HD_FOX)
    p_fox_logf = lfT.transpose(0, 2, 1)[None]
    p_mem_k = heads_out(mkT, H_MEM, HD_MEM)
    p_mem_v = heads_out(mvT, H_MEM, HD_MEM)

    gla_o_s, s_state = sample_gla(_lane_rep(s_gq), _lane_rep(s_gk), _lane_rep(s_ga), s_gv[:, None, :],
                                  s_gr[:, None, :], state_gla[0], bfT(w_gla_gate_up[0]), col(b_gla_gate[0]),
                                  row(g_gla_norm[0]))
    mem_T = lambda c: c.transpose(0, 2, 3, 1).reshape(c.shape[0], W_MEM, c.shape[1])
    xn_rows = jnp.broadcast_to(xn_s[:, None, :], (DB, SAMPLE_MEM_ROWS, D))
    mem_o_s = mem_attn(xn_rows, bf(w_mq), mem_T(cache_mem_k[0]), mem_T(cache_mem_v[0]), SAMPLE_MEM_ROWS)[:, 0]
    h_s, _ = merge(xs, xn_s, bf(fox_o_s.reshape(DB, W_FOX)), bf(gla_o_s.reshape(DB, W_GLA_V)), mem_o_s,
                   *merge_w, DB)
    y_sample, _ = mlp(h_s, *mlp_w, DB)
    y_sample = y_sample.reshape(DB, 1, D)

    s_fox_k = s_fk.reshape(1, DB, 1, H_FOX, HD_FOX)
    s_fox_v = s_fv.reshape(1, DB, 1, H_FOX, HD_FOX)
    s_fox_logf = lf_new.reshape(1, DB, 1, H_FOX)
    return (y_prompt, y_sample, p_fox_k, p_fox_v, p_fox_logf, p_state[None], p_mem_k, p_mem_v,
            s_fox_k, s_fox_v, s_fox_logf, s_state[None])
```

```python
import functools
from typing import NamedTuple

import jax
import jax.numpy as jnp
from jax import lax
from jax.experimental import pallas as pl
from jax.experimental.pallas import tpu as pltpu

F32 = jnp.float32
BF16 = jnp.bfloat16

H_FOX = 16
HD_FOX = 64
W_FOX = H_FOX * HD_FOX
H_GLA = 4
DK_GLA = 128
DV_GLA = 256
W_GLA_K = H_GLA * DK_GLA
W_GLA_V = H_GLA * DV_GLA
GLA_RANK = 16
GLA_NORMALIZER = 16.0
GLA_CHUNK = 64
MEM_LEN = 256
H_MEM = 4
HD_MEM = 64
W_MEM = H_MEM * HD_MEM
N_BRANCH = 3
EPS = 1e-6

LANES = 128
FOX_AUG = 128
VMEM_LIMIT = 56 * 1024 * 1024


def _dot(a, b):
    return jnp.dot(a, b, preferred_element_type=F32)


def _dot_nt(a, b):
    return lax.dot_general(a, b, (((1,), (1,)), ((), ())), preferred_element_type=F32)


def _split_bf16(x, parts):
    out = []
    r = x
    for _ in range(parts):
        p = r.astype(BF16)
        out.append(p)
        r = r - p.astype(F32)
    return out


def _log_sigmoid(x):
    return jnp.minimum(x, 0.0) - jnp.log1p(jnp.exp(-jnp.abs(x)))


def _sigmoid(x):
    return 1.0 / (1.0 + jnp.exp(-x))


def _rmsnorm(x, g):
    ms = jnp.mean(x * x, axis=-1, keepdims=True)
    return x * lax.rsqrt(ms + EPS) * g


def _lane_cumsum(x, reverse=False):
    rows, n = x.shape
    lane = lax.broadcasted_iota(jnp.int32, (rows, LANES), 1)
    blocks = []
    for i in range(n // LANES):
        y = x[:, i * LANES:(i + 1) * LANES]
        k = 1
        while k < LANES:
            if reverse:
                y = y + jnp.where(lane < LANES - k, pltpu.roll(y, LANES - k, 1), 0.0)
            else:
                y = y + jnp.where(lane >= k, pltpu.roll(y, k, 1), 0.0)
            k *= 2
        blocks.append(y)
    order = range(len(blocks) - 1, -1, -1) if reverse else range(len(blocks))
    edge = 0 if reverse else LANES - 1
    carry = None
    for i in order:
        if carry is not None:
            blocks[i] = blocks[i] + carry
        carry = blocks[i][:, edge:edge + 1]
    return blocks[0] if len(blocks) == 1 else jnp.concatenate(blocks, axis=1)


def _const_spec(shape):
    return pl.BlockSpec(shape, lambda *_: (0,) * len(shape), pipeline_mode=pl.Buffered(1))


def _params(sem):
    return pltpu.CompilerParams(dimension_semantics=sem, vmem_limit_bytes=VMEM_LIMIT)


def _split3_f32(x):
    hi = x.astype(BF16).astype(F32)
    r = x - hi
    mid = r.astype(BF16).astype(F32)
    lo = (r - mid).astype(BF16).astype(F32)
    return hi, mid, lo


def _fox_proj_kernel(x_ref, g_ref, wqT_ref, wk_ref, wvT_ref, wffT_ref, bcol_ref, e_ref,
                     xn_ref, qT_ref, ka_ref, kT_ref, vT_ref, vT16_ref, lfT_ref, cT_ref, carry_ref):
    i = pl.program_id(1)
    ts = x_ref.shape[1]

    @pl.when(i == 0)
    def _():
        carry_ref[...] = jnp.zeros_like(carry_ref)

    xn = _rmsnorm(x_ref[0], g_ref[...]).astype(BF16)
    xn_ref[0] = xn
    qT_ref[0] = (_dot_nt(wqT_ref[...], xn) * (HD_FOX ** -0.5)).astype(BF16)
    vT = _dot_nt(wvT_ref[...], xn)
    vT_ref[0] = vT
    vT16_ref[0] = vT.astype(BF16)
    k = _dot(xn, wk_ref[...])
    kT_ref[0] = k.T

    lfT = _log_sigmoid(_dot_nt(wffT_ref[...], xn) + bcol_ref[...])
    lfT_ref[0] = lfT
    cT = _lane_cumsum(lfT) + carry_ref[...]
    carry_ref[...] = cT[:, ts - 1:ts]
    cT_ref[0] = cT

    hi, mid, lo = _split3_f32(-cT)
    stack = jnp.concatenate([hi, mid, lo, jnp.ones((H_FOX, ts), F32),
                             jnp.zeros((LANES - 4 * H_FOX, ts), F32)], axis=0)
    ex = _dot(stack.T.astype(BF16), e_ref[...])
    lower = lax.broadcasted_iota(jnp.int32, (ts, LANES), 1) < HD_FOX
    for g in range(H_FOX // 2):
        kg = k[:, g * LANES:(g + 1) * LANES]
        eg = ex[:, g * LANES:(g + 1) * LANES]
        ka_ref[0, :, (2 * g) * FOX_AUG:(2 * g + 1) * FOX_AUG] = jnp.where(lower, kg, eg).astype(BF16)
        ka_ref[0, :, (2 * g + 1) * FOX_AUG:(2 * g + 2) * FOX_AUG] = jnp.where(lower, eg, kg).astype(BF16)


def _fox_bias_selector():
    e = jnp.zeros((LANES, W_FOX), F32)
    for h in range(H_FOX):
        base = (h // 2) * LANES + (HD_FOX if h % 2 == 0 else 0)
        for j in range(3):
            e = e.at[3 * H_FOX, base + j].set(1.0)
            e = e.at[j * H_FOX + h, base + 3 + j].set(1.0)
    return e.astype(BF16)


def fox_proj(x, g, wqT, wk, wvT, wffT, bcol, ts):
    B, S, D = x.shape
    e = _fox_bias_selector()
    out_shape = (
        jax.ShapeDtypeStruct((B, S, D), BF16),
        jax.ShapeDtypeStruct((B, W_FOX, S), BF16),
        jax.ShapeDtypeStruct((B, S, H_FOX * FOX_AUG), BF16),
        jax.ShapeDtypeStruct((B, W_FOX, S), F32),
        jax.ShapeDtypeStruct((B, W_FOX, S), F32),
        jax.ShapeDtypeStruct((B, W_FOX, S), BF16),
        jax.ShapeDtypeStruct((B, H_FOX, S), F32),
        jax.ShapeDtypeStruct((B, H_FOX, S), F32),
    )
    tile = lambda n: pl.BlockSpec((1, ts, n), lambda b, i: (b, i, 0))
    tileT = lambda n: pl.BlockSpec((1, n, ts), lambda b, i: (b, 0, i))
    consts = (g, wqT, wk, wvT, wffT, bcol, e)
    return pl.pallas_call(
        _fox_proj_kernel,
        grid=(B, S // ts),
        in_specs=[tile(D)] + [_const_spec(c.shape) for c in consts],
        out_specs=(tile(D), tileT(W_FOX), tile(H_FOX * FOX_AUG), tileT(W_FOX), tileT(W_FOX),
                   tileT(W_FOX), tileT(H_FOX), tileT(H_FOX)),
        out_shape=out_shape,
        scratch_shapes=[pltpu.VMEM((H_FOX, 1), F32)],
        compiler_params=_params(("parallel", "arbitrary")),
        name="fox_proj",
    )(x, *consts)


L_ROWS = 16
FOX_HEAD_GROUP = 4


def _fox_attn_kernel(qi_ref, kj_ref, qT_ref, cT_ref, ka_ref, vT_ref, o_ref,
                     qa_ref, s_ref, mx_ref, m_ref, acc_ref, *, tq):
    group = pl.program_id(1)
    n_heads = qa_ref.shape[0]
    S = qT_ref.shape[2]
    tk = tq
    n_pairs = qi_ref.shape[0]

    row8 = lax.broadcasted_iota(jnp.int32, (8, S), 0)
    for h in range(n_heads):
        hi, mid, lo = _split3_f32(cT_ref[0, pl.ds(n_heads * group + h, 1), :])
        ext8 = jnp.where(row8 == 0, hi, jnp.where(row8 == 1, mid, jnp.where(
            row8 == 2, lo, jnp.where(row8 < 6, 1.0, 0.0))))
        ext = jnp.concatenate([ext8, jnp.zeros((HD_FOX - 8, S), F32)], axis=0).astype(BF16)
        qh = qT_ref[0, h * HD_FOX:(h + 1) * HD_FOX, :]
        qa_ref[h] = jnp.concatenate([qh, ext] if h % 2 == 0 else [ext, qh], axis=0)
        m_ref[h] = jnp.full((1, tq), -jnp.inf, F32)
        acc_ref[h] = jnp.zeros((HD_FOX + L_ROWS, tq), F32)

    krow = lax.broadcasted_iota(jnp.int32, (tk, tq), 0)
    qcol = lax.broadcasted_iota(jnp.int32, (tk, tq), 1)
    causal = krow <= qcol
    ones = jnp.ones((L_ROWS, tk), BF16)

    def scores(t, slot, h):
        qoff = pl.multiple_of(qi_ref[t] * tq, tq)
        koff = pl.multiple_of(kj_ref[t] * tk, tk)
        sT = _dot(ka_ref[0, pl.ds(koff, tk), h * FOX_AUG:(h + 1) * FOX_AUG],
                  qa_ref[h, :, pl.ds(qoff, tq)])
        s_ref[slot, h] = sT
        mx_ref[slot, h] = jnp.max(sT, axis=0, keepdims=True)

    def consume(t, slot, h, diagonal):
        koff = pl.multiple_of(kj_ref[t] * tk, tk)
        sT = s_ref[slot, h]
        if diagonal:
            sT = jnp.where(causal, sT, -jnp.inf)
            mx = jnp.max(sT, axis=0, keepdims=True)
        else:
            mx = mx_ref[slot, h]
        m_prev = m_ref[h]
        m_next = jnp.maximum(m_prev, mx)
        pT = jnp.exp(sT - m_next).astype(BF16)
        alpha = jnp.exp(m_prev - m_next)
        va = jnp.concatenate([vT_ref[0, h * HD_FOX:(h + 1) * HD_FOX, pl.ds(koff, tk)], ones], axis=0)
        acc = alpha * acc_ref[h] + _dot(va, pT)
        if not diagonal:
            m_ref[h] = m_next
            acc_ref[h] = acc
            return None
        m_ref[h] = jnp.full((1, tq), -jnp.inf, F32)
        acc_ref[h] = jnp.zeros((HD_FOX + L_ROWS, tq), F32)
        return acc[:HD_FOX] / acc[HD_FOX:HD_FOX + 1]

    def trip(t, slot, diagonal):
        nxt = jnp.minimum(t + 1, n_pairs - 1)
        outs = []
        for h in range(n_heads):
            scores(nxt, 1 - slot, h)
            outs.append(consume(t, slot, h, diagonal))
        if diagonal:
            qoff = pl.multiple_of(qi_ref[t] * tq, tq)
            o_ref[0, pl.ds(qoff, tq), :] = jnp.concatenate(outs, axis=0).T.astype(o_ref.dtype)

    def either(t, slot):
        is_diag = kj_ref[t] == qi_ref[t]

        @pl.when(is_diag)
        def _():
            trip(t, slot, True)

        @pl.when(jnp.logical_not(is_diag))
        def _():
            trip(t, slot, False)

    for h in range(n_heads):
        scores(0, 0, h)

    def body(u, carry):
        either(2 * u, 0)
        either(2 * u + 1, 1)
        return carry

    lax.fori_loop(0, n_pairs // 2, body, 0)
    if n_pairs % 2:
        either(n_pairs - 1, 0)


def fox_attn(qT, cT, ka, vT16, tq):
    B, _, S = qT.shape
    nq = S // tq
    qi = jnp.asarray([i for i in range(nq) for _ in range(i + 1)], jnp.int32)
    kj = jnp.asarray([j for i in range(nq) for j in range(i + 1)], jnp.int32)
    hg = FOX_HEAD_GROUP
    grid_spec = pltpu.PrefetchScalarGridSpec(
        num_scalar_prefetch=2,
        grid=(B, H_FOX // hg),
        in_specs=[pl.BlockSpec((1, hg * HD_FOX, S), lambda b, p, *_: (b, p, 0)),
                  pl.BlockSpec((1, H_FOX, S), lambda b, p, *_: (b, 0, 0)),
                  pl.BlockSpec((1, S, hg * FOX_AUG), lambda b, p, *_: (b, 0, p)),
                  pl.BlockSpec((1, hg * HD_FOX, S), lambda b, p, *_: (b, p, 0))],
        out_specs=pl.BlockSpec((1, S, hg * HD_FOX), lambda b, p, *_: (b, 0, p)),
        scratch_shapes=[pltpu.VMEM((hg, FOX_AUG, S), BF16),
                        pltpu.VMEM((2, hg, tq, tq), F32),
                        pltpu.VMEM((2, hg, 1, tq), F32),
                        pltpu.VMEM((hg, 1, tq), F32),
                        pltpu.VMEM((hg, HD_FOX + L_ROWS, tq), F32)],
    )
    return pl.pallas_call(
        functools.partial(_fox_attn_kernel, tq=tq),
        grid_spec=grid_spec,
        out_shape=jax.ShapeDtypeStruct((B, S, W_FOX), BF16),
        compiler_params=_params(("parallel", "parallel")),
        name="fox_attn",
    )(qi, kj, qT, cT, ka, vT16)


def _gla_kernel(xn_ref, wqT_ref, wkT_ref, wv_ref, waT_ref, wr_ref, wuT_ref, bcol_ref, g_ref,
                o_ref, st_ref, s_ref, oacc_ref):
    t = pl.program_id(1)
    T = xn_ref.shape[1]
    C = GLA_CHUNK

    @pl.when(t == 0)
    def _():
        s_ref[...] = jnp.zeros_like(s_ref)

    xn = xn_ref[0]
    qT = _dot_nt(wqT_ref[...], xn) * (DK_GLA ** -0.5)
    kT = _dot_nt(wkT_ref[...], xn)
    v = _dot(xn, wv_ref[...]).astype(BF16)
    gaT = _dot_nt(waT_ref[...], xn).astype(BF16)
    laT = _log_sigmoid(_dot(wuT_ref[...], gaT) + bcol_ref[...]) / GLA_NORMALIZER

    r = lax.broadcasted_iota(jnp.int32, (T, T), 0)
    c_ = lax.broadcasted_iota(jnp.int32, (T, T), 1)
    same = (r // C) == (c_ // C)
    upper = jnp.logical_and(same, r <= c_).astype(BF16)
    block = same.astype(BF16)
    bT = totT = None
    for part in _split_bf16(laT, 2):
        d = _dot(part, upper)
        bT = d if bT is None else bT + d
        d = _dot(part, block)
        totT = d if totT is None else totT + d

    qe = (qT * jnp.exp(bT)).T.astype(BF16)
    kdT = (kT * jnp.exp(-bT)).astype(BF16)
    keT = (kT * jnp.exp(totT - bT)).astype(BF16)
    decT = jnp.exp(totT)
    in_chunk_causal = jnp.logical_and(same, c_ <= r)

    n_chunks = T // C
    for h in range(H_GLA):
        kk = slice(h * DK_GLA, (h + 1) * DK_GLA)
        vv = slice(h * DV_GLA, (h + 1) * DV_GLA)
        a = jnp.where(in_chunk_causal, _dot(qe[:, kk], kdT[kk, :]), 0.0)
        o_intra = _dot(a.astype(BF16), v[:, vv])
        ds = [_dot(keT[kk, c * C:(c + 1) * C], v[c * C:(c + 1) * C, vv]) for c in range(n_chunks)]
        s = s_ref[h]
        for c in range(n_chunks):
            rows = slice(c * C, (c + 1) * C)
            oacc_ref[rows, vv] = o_intra[rows] + _dot(qe[rows, kk], s.astype(BF16))
            s = decT[kk, c * C:c * C + 1] * s + ds[c]
        s_ref[h] = s

    gr = _dot(xn, wr_ref[...])
    gate = gr * _sigmoid(gr)
    for h in range(H_GLA):
        vv = slice(h * DV_GLA, (h + 1) * DV_GLA)
        y = _rmsnorm(oacc_ref[:, vv], g_ref[...])
        o_ref[0, :, vv] = (y * gate[:, vv]).astype(o_ref.dtype)

    @pl.when(t == pl.num_programs(1) - 1)
    def _():
        st_ref[0] = s_ref[...]


def gla(xn, wqT, wkT, wv, waT, wr, wuT, bcol, g, T):
    B, S, D = xn.shape
    consts = (wqT, wkT, wv, waT, wr, wuT, bcol, g)
    return pl.pallas_call(
        _gla_kernel,
        grid=(B, S // T),
        in_specs=[pl.BlockSpec((1, T, D), lambda b, t: (b, t, 0))] + [_const_spec(w.shape) for w in consts],
        out_specs=(pl.BlockSpec((1, T, W_GLA_V), lambda b, t: (b, t, 0)),
                   pl.BlockSpec((1, H_GLA, DK_GLA, DV_GLA), lambda b, t: (b, 0, 0, 0))),
        out_shape=(jax.ShapeDtypeStruct((B, S, W_GLA_V), BF16),
                   jax.ShapeDtypeStruct((B, H_GLA, DK_GLA, DV_GLA), F32)),
        scratch_shapes=[pltpu.VMEM((H_GLA, DK_GLA, DV_GLA), F32), pltpu.VMEM((T, W_GLA_V), F32)],
        compiler_params=_params(("parallel", "arbitrary")),
        name="gla",
    )(xn, *consts)


def _mem_kv_kernel(m_ref, g_ref, wkT_ref, wvT_ref, kT_ref, vT_ref):
    mn = _rmsnorm(m_ref[0], g_ref[...]).astype(BF16)
    kT_ref[0] = _dot_nt(wkT_ref[...], mn)
    vT_ref[0] = _dot_nt(wvT_ref[...], mn)


def mem_kv(mem, g, wkT, wvT):
    B, M, D = mem.shape
    blk = pl.BlockSpec((1, W_MEM, M), lambda b: (b, 0, 0))
    return pl.pallas_call(
        _mem_kv_kernel,
        grid=(B,),
        in_specs=[pl.BlockSpec((1, M, D), lambda b: (b, 0, 0)), _const_spec(g.shape),
                  _const_spec(wkT.shape), _const_spec(wvT.shape)],
        out_specs=(blk, blk),
        out_shape=(jax.ShapeDtypeStruct((B, W_MEM, M), F32),) * 2,
        compiler_params=_params(("parallel",)),
        name="mem_kv",
    )(mem, g, wkT, wvT)


def _mem_attn_kernel(xn_ref, wq_ref, kT_ref, vT_ref, o_ref):
    q = _dot(xn_ref[0], wq_ref[...]).astype(BF16)
    outs = []
    for h in range(H_MEM):
        hh = slice(h * HD_MEM, (h + 1) * HD_MEM)
        s = _dot(q[:, hh], kT_ref[0, hh, :].astype(BF16)) * (HD_MEM ** -0.5)
        p = jnp.exp(s - jnp.max(s, axis=1, keepdims=True))
        p = p / jnp.sum(p, axis=1, keepdims=True)
        outs.append(_dot_nt(p.astype(BF16), vT_ref[0, hh, :].astype(BF16)))
    o_ref[0] = jnp.concatenate(outs, axis=1).astype(o_ref.dtype)


def mem_attn(xn, wq, kT, vT, tl):
    B, L, D = xn.shape
    M = kT.shape[2]
    return pl.pallas_call(
        _mem_attn_kernel,
        grid=(B, L // tl),
        in_specs=[pl.BlockSpec((1, tl, D), lambda b, i: (b, i, 0)), _const_spec(wq.shape),
                  pl.BlockSpec((1, W_MEM, M), lambda b, i: (b, 0, 0)),
                  pl.BlockSpec((1, W_MEM, M), lambda b, i: (b, 0, 0))],
        out_specs=pl.BlockSpec((1, tl, W_MEM), lambda b, i: (b, i, 0)),
        out_shape=jax.ShapeDtypeStruct((B, L, W_MEM), BF16),
        compiler_params=_params(("parallel", "parallel")),
        name="mem_attn",
    )(xn, wq, kT, vT)


HOST_CHUNKS = 4


def _run_host(n_own_in, n_own_out, chunk_fn, finish_fn, refs, plan):
    if plan is None:
        for c in range(HOST_CHUNKS):
            chunk_fn(c)
        finish_fn()
        return
    assert plan.groups == HOST_CHUNKS
    pt_ref, refs = refs[0], refs[1:]
    dec_in = refs[n_own_in:n_own_in + 7]
    dec_out = refs[n_own_in + 7 + n_own_out]
    dec_scratch = refs[n_own_in + 7 + n_own_out + 1:]
    _decode_side_task(plan, pl.program_id(0), pl.num_programs(0),
                      (pt_ref, *dec_in, dec_out, *dec_scratch), chunk_fn)
    finish_fn()


def _host_call(kernel, name, n_rows, tm, own_in_specs, own_out_spec, own_out_shape, own_args, decode):
    if decode is None:
        return pl.pallas_call(
            functools.partial(kernel, plan=None), grid=(n_rows // tm,), in_specs=own_in_specs,
            out_specs=own_out_spec, out_shape=own_out_shape, compiler_params=_params(("parallel",)),
            name=name)(*own_args), None
    plan, page_table, dec_args = decode
    dec_in_specs, dec_out_spec, dec_out_shape, dec_scratch = _decode_host_specs(plan)
    grid_spec = pltpu.PrefetchScalarGridSpec(
        num_scalar_prefetch=1, grid=(n_rows // tm,),
        in_specs=list(own_in_specs) + dec_in_specs,
        out_specs=(own_out_spec, dec_out_spec),
        scratch_shapes=dec_scratch)
    return pl.pallas_call(
        functools.partial(kernel, plan=plan), grid_spec=grid_spec,
        out_shape=(own_out_shape, dec_out_shape),
        compiler_params=_params(("arbitrary",)), name=name + "_decode")(page_table, *own_args, *dec_args)


def _merge_kernel(*refs, plan):
    own = refs[1:] if plan is not None else refs
    x_ref, xn_ref, fo_ref, go_ref, mo_ref, wgl_ref, wuf_ref, wug_ref, wum_ref, wout_ref = own[:10]
    h_ref = own[10 + (7 if plan is not None else 0)]
    D = x_ref.shape[1]
    branches = ((fo_ref, wuf_ref), (go_ref, wug_ref), (mo_ref, wum_ref))
    merged = []

    def chunk(c):
        if c < N_BRANCH:
            br, w = branches[c]
            gate = _sigmoid(_dot(xn_ref[...], wgl_ref[:, c * D:(c + 1) * D]))
            merged.append(gate * _dot(br[...], w[...]))
        else:
            m = (merged[0] + merged[1] + merged[2]).astype(BF16)
            h_ref[...] = x_ref[...] + _dot(m, wout_ref[...])

    _run_host(10, 1, chunk, lambda: None, refs, plan)


def merge(x, xn, fo, go, mo, wgl, wuf, wug, wum, wout, tm, decode=None):
    N, D = x.shape
    row = lambda n: pl.BlockSpec((tm, n), lambda i, *_: (i, 0))
    consts = (wgl, wuf, wug, wum, wout)
    in_specs = [row(D), row(D), row(fo.shape[1]), row(go.shape[1]), row(mo.shape[1])] \
        + [_const_spec(w.shape) for w in consts]
    return _host_call(_merge_kernel, "merge", N, tm, in_specs, row(D), jax.ShapeDtypeStruct((N, D), F32),
                      (x, xn, fo, go, mo, *consts), decode)


def _mlp_kernel(*refs, plan):
    own = refs[1:] if plan is not None else refs
    h_ref, g_ref, w1_ref, w2_ref, gf_ref = own[:5]
    y_ref = own[5 + (7 if plan is not None else 0)]
    fc = w1_ref.shape[1] // HOST_CHUNKS
    state = {}

    def chunk(c):
        if c == 0:
            state["acc"] = h_ref[...]
            state["hn"] = _rmsnorm(state["acc"], g_ref[...]).astype(BF16)
        cc = slice(c * fc, (c + 1) * fc)
        u = jnp.maximum(_dot(state["hn"], w1_ref[:, cc]), 0.0)
        state["acc"] = state["acc"] + _dot((u * u).astype(BF16), w2_ref[cc, :])

    def finish():
        y_ref[...] = _rmsnorm(state["acc"], gf_ref[...])

    _run_host(5, 1, chunk, finish, refs, plan)


def mlp(h, g, w1, w2, gf, tm, decode=None):
    N, D = h.shape
    row = pl.BlockSpec((tm, D), lambda i, *_: (i, 0))
    in_specs = [row, _const_spec(g.shape), _const_spec(w1.shape), _const_spec(w2.shape), _const_spec(gf.shape)]
    return _host_call(_mlp_kernel, "mlp", N, tm, in_specs, row, jax.ShapeDtypeStruct((N, D), F32),
                      (h, g, w1, w2, gf), decode)


def _sample_proj_kernel(x_ref, g_ref, w_ref, xn_ref, o_ref):
    xn = _rmsnorm(x_ref[...], g_ref[...]).astype(BF16)
    xn_ref[...] = xn
    o_ref[...] = _dot(xn, w_ref[...])


def sample_proj(x, g, w, tn):
    N, D = x.shape
    NP = w.shape[1]
    return pl.pallas_call(
        _sample_proj_kernel,
        grid=(NP // tn,),
        in_specs=[_const_spec((N, D)), _const_spec(g.shape), pl.BlockSpec((D, tn), lambda j: (0, j))],
        out_specs=(pl.BlockSpec((N, D), lambda j: (0, 0)), pl.BlockSpec((N, tn), lambda j: (0, j))),
        out_shape=(jax.ShapeDtypeStruct((N, D), BF16), jax.ShapeDtypeStruct((N, NP), F32)),
        compiler_params=_params(("arbitrary",)),
        name="sample_proj",
    )(x, g, w)


def _logsig_kernel(ff_ref, b_ref, o_ref):
    o_ref[...] = _log_sigmoid(ff_ref[...] + b_ref[...])


def logsig_bias(ff, b):
    return pl.pallas_call(_logsig_kernel, out_shape=jax.ShapeDtypeStruct(ff.shape, F32), name="logsig")(ff, b)


RING_AHEAD = 3


class _DecodePlan(NamedTuple):
    first_b: int
    n_b: int
    steps_per_b: int
    groups: int
    pages: int
    n_pages: int


def _plan_decode(first_b, n_b, n_steps, n_pages, pages):
    steps_per_b, rem = divmod(n_steps, n_b)
    assert rem == 0 and steps_per_b >= 1, (n_steps, n_b)
    per_step, rem = divmod(n_pages, steps_per_b)
    assert rem == 0, (n_pages, steps_per_b)
    groups, rem = divmod(per_step, pages)
    assert rem == 0 and groups > RING_AHEAD, (per_step, pages)
    return _DecodePlan(first_b, n_b, steps_per_b, groups, pages, n_pages)


def _decode_side_task(plan, i, n_steps, refs, host_chunk):
    (pt_ref, q_ref, kn_ref, vn_ref, lfn_ref, k_hbm, v_hbm, lf_hbm, o_ref,
     kbuf, vbuf, lfbuf, sem, qs_ref, m_ref, l_ref, r_ref, acc_ref) = refs
    G = plan.pages
    part = i % plan.steps_per_b
    lane = lax.broadcasted_iota(jnp.int32, (H_FOX, LANES), 1)
    lane_w = lax.broadcasted_iota(jnp.int32, (W_FOX, LANES), 1)
    suffix_tri = (lax.broadcasted_iota(jnp.int32, (LANES, LANES), 0)
                  >= lax.broadcasted_iota(jnp.int32, (LANES, LANES), 1)).astype(BF16)

    def group_copies(step, g, slot):
        b = plan.first_b + step // plan.steps_per_b
        done = (step % plan.steps_per_b) * plan.groups * G + g * G
        copies = []
        for k in range(G):
            page = pt_ref[b, plan.n_pages - 1 - (done + k)]
            copies += [pltpu.make_async_copy(k_hbm.at[page], kbuf.at[slot, k], sem.at[slot, 0]),
                       pltpu.make_async_copy(v_hbm.at[page], vbuf.at[slot, k], sem.at[slot, 1]),
                       pltpu.make_async_copy(lf_hbm.at[page], lfbuf.at[slot, k], sem.at[slot, 2])]
        return copies

    def head_scores(k_ref):
        return jnp.concatenate(
            [jnp.sum(k_ref[h * HD_FOX:(h + 1) * HD_FOX, :] * qs_ref[h * HD_FOX:(h + 1) * HD_FOX, :],
                     axis=0, keepdims=True) for h in range(H_FOX)], axis=0)

    def start_group(target):
        step_off, g_t = divmod(target, plan.groups)
        if step_off == 0:
            for c in group_copies(i, g_t, g_t):
                c.start()
        else:
            @pl.when(i + step_off < n_steps)
            def _():
                for c in group_copies(i + step_off, g_t, g_t):
                    c.start()

    @pl.when(i == 0)
    def _():
        for g0 in range(RING_AHEAD):
            for c in group_copies(0, g0, g0):
                c.start()

    @pl.when(part == 0)
    def _():
        qs_ref[...] = q_ref[0] * (HD_FOX ** -0.5)
        m_ref[...] = head_scores(kn_ref.at[0])
        l_ref[...] = jnp.where(lane == 0, 1.0, 0.0)
        r_ref[...] = jnp.zeros_like(r_ref)
        acc_ref[...] = jnp.where(lane_w == 0, vn_ref[0], 0.0)

    for g in range(plan.groups):
        slot = g
        start_group(g + RING_AHEAD)
        host_chunk(g)
        for c in group_copies(i, g, slot):
            c.wait()

        lf_all = lfbuf[slot].reshape(G * H_FOX, LANES)
        incl_all = None
        for piece in _split_bf16(lf_all, 3):
            t = _dot(piece, suffix_tri)
            incl_all = t if incl_all is None else incl_all + t
        r_run = r_ref[...]
        scores = []
        for k in range(G):
            lf = lf_all[k * H_FOX:(k + 1) * H_FOX]
            incl = incl_all[k * H_FOX:(k + 1) * H_FOX]
            scores.append(head_scores(kbuf.at[slot, k]) + lfn_ref[0] + (incl - lf + r_run))
            r_run = r_run + incl[:, 0:1]
        r_ref[...] = r_run
        m_prev = m_ref[...]
        m_new = m_prev
        for s in scores:
            m_new = jnp.maximum(m_new, jnp.max(s, axis=1, keepdims=True))
        alpha = jnp.exp(m_prev - m_new)
        probs = [jnp.exp(s - m_new) for s in scores]
        l_ref[...] = alpha * l_ref[...] + functools.reduce(lambda a, b: a + b, probs)
        m_ref[...] = m_new
        for h in range(H_FOX):
            rows = slice(h * HD_FOX, (h + 1) * HD_FOX)
            a = acc_ref[rows, :] * alpha[h:h + 1, :]
            for k in range(G):
                a = a + vbuf[slot, k, rows, :] * probs[k][h:h + 1, :]
            acc_ref[rows, :] = a

    @pl.when(part == plan.steps_per_b - 1)
    def _():
        inv = 1.0 / jnp.sum(l_ref[...], axis=1, keepdims=True)
        for h in range(H_FOX):
            rows = slice(h * HD_FOX, (h + 1) * HD_FOX)
            acc_ref[rows, :] = acc_ref[rows, :] * inv[h:h + 1, :]
        ones = jnp.ones((8, LANES), BF16)
        out = None
        for piece in _split_bf16(acc_ref[...], 3):
            t = _dot_nt(ones, piece)
            out = t if out is None else out + t
        o_ref[0] = out[0:1, :]


def _decode_host_specs(plan):
    b_of = lambda i: plan.first_b + i // plan.steps_per_b
    per_b = lambda n: pl.BlockSpec((1, n, LANES), lambda i, pt: (b_of(i), 0, 0))
    pool = pl.BlockSpec(memory_space=pl.ANY)
    in_specs = [per_b(W_FOX), per_b(W_FOX), per_b(W_FOX), per_b(H_FOX), pool, pool, pool]
    out_spec = pl.BlockSpec((1, 1, W_FOX), lambda i, pt: (i // plan.steps_per_b, 0, 0))
    out_shape = jax.ShapeDtypeStruct((plan.n_b, 1, W_FOX), F32)
    G, slots = plan.pages, plan.groups
    scratch = [pltpu.VMEM((slots, G, W_FOX, LANES), F32), pltpu.VMEM((slots, G, W_FOX, LANES), F32),
               pltpu.VMEM((slots, G, H_FOX, LANES), F32), pltpu.SemaphoreType.DMA((slots, 3)),
               pltpu.VMEM((W_FOX, LANES), F32), pltpu.VMEM((H_FOX, LANES), F32),
               pltpu.VMEM((H_FOX, LANES), F32), pltpu.VMEM((H_FOX, LANES), F32),
               pltpu.VMEM((W_FOX, LANES), F32)]
    return in_specs, out_spec, out_shape, scratch


def _sample_gla_kernel(q_ref, k_ref, ga_ref, v_ref, gr_ref, s0_ref, wuT_ref, bcol_ref, g_ref, o_ref, st_ref):
    z = _dot(wuT_ref[...], ga_ref[0].astype(BF16)) + bcol_ref[...]
    la = _log_sigmoid(z) / GLA_NORMALIZER
    ea = jnp.exp(la)
    k = k_ref[0]
    qe = q_ref[0] * (DK_GLA ** -0.5) * ea
    kd = k * jnp.exp(-la)
    gr = gr_ref[0]
    gate = gr * _sigmoid(gr)
    wide = lambda a: jnp.concatenate([a, a], axis=1)
    for h in range(H_GLA):
        kk = slice(h * DK_GLA, (h + 1) * DK_GLA)
        vv = slice(h * DV_GLA, (h + 1) * DV_GLA)
        s_old = s0_ref[0, h]
        v_h = v_ref[0][:, vv]
        a = jnp.sum(qe[kk] * kd[kk], axis=0, keepdims=True)
        o = wide(a) * v_h + jnp.sum(wide(qe[kk]) * s_old, axis=0, keepdims=True)
        st_ref[0, h] = wide(ea[kk]) * s_old + wide(k[kk]) * v_h
        o_ref[0, :, vv] = _rmsnorm(o, g_ref[...]) * gate[:, vv]


def sample_gla(q_rep, k_rep, ga_rep, v, gr, s0, wuT, bcol, g):
    DB = q_rep.shape[0]
    per_b = lambda a: pl.BlockSpec((1,) + a.shape[1:], lambda b: (b,) + (0,) * (a.ndim - 1))
    return pl.pallas_call(
        _sample_gla_kernel,
        grid=(DB,),
        in_specs=[per_b(q_rep), per_b(k_rep), per_b(ga_rep), per_b(v), per_b(gr), per_b(s0),
                  _const_spec(wuT.shape), _const_spec(bcol.shape), _const_spec(g.shape)],
        out_specs=(per_b(v), per_b(s0)),
        out_shape=(jax.ShapeDtypeStruct(v.shape, F32), jax.ShapeDtypeStruct(s0.shape, F32)),
        compiler_params=_params(("parallel",)),
        name="sample_gla",
    )(q_rep, k_rep, ga_rep, v, gr, s0, wuT, bcol, g)


_SPLITS = (W_FOX, W_FOX, W_FOX, H_FOX, W_GLA_K, W_GLA_K, W_GLA_V, GLA_RANK, W_GLA_V, W_MEM)
PROMPT_TILE = 512
GLA_TILE = 256
ROW_TILE = 256
DECODE_PAGES = 4
SAMPLE_PROJ_TILE = 512
SAMPLE_MEM_ROWS = 8


def _lane_rep(a):
    return jnp.broadcast_to(a[..., None], a.shape + (LANES,))


def kernel(x_prompt, x_sample, mem_prompt, cache_fox_k, cache_fox_v, cache_fox_logf, state_gla, cache_mem_k, cache_mem_v, page_table, g_attn, w_in, b_fox_f, w_gla_gate_up, b_gla_gate, g_gla_norm, g_mem, w_mem_k, w_mem_v, w_up_fox, w_up_gla, w_up_mem, w_out, g_mlp, w_mlp_in, w_mlp_out, g_final):
    B, S, D = x_prompt.shape
    DB = x_sample.shape[0]
    assert w_in.shape[0] == 1 and x_sample.shape[1] == 1

    w = w_in[0]
    segs, off = [], 0
    for n in _SPLITS:
        segs.append(w[:, off:off + n])
        off += n
    w_fq, w_fk, w_fv, w_ff, w_gq, w_gk, w_gv, w_ga, w_gr, w_mq = segs
    w_gl = w[:, off:]
    bf = lambda a: a.astype(BF16)
    bfT = lambda a: a.T.astype(BF16)
    row = lambda a: a.reshape(1, -1)
    col = lambda a: a.reshape(-1, 1)

    ga_args = (bfT(w_gq), bfT(w_gk), bf(w_gv), bfT(w_ga), bf(w_gr), bfT(w_gla_gate_up[0]),
               col(b_gla_gate[0]), row(g_gla_norm[0]))
    merge_w = (bf(w_gl), bf(w_up_fox[0]), bf(w_up_gla[0]), bf(w_up_mem[0]), bf(w_out[0]))
    mlp_w = (row(g_mlp[0]), bf(w_mlp_in[0]), bf(w_mlp_out[0]), row(g_final))
    g_attn_row = row(g_attn[0])

    xs = x_sample.reshape(DB, D)
    d_in = w.shape[1]
    d_pad = -(-d_in // SAMPLE_PROJ_TILE) * SAMPLE_PROJ_TILE
    xn_s, proj = sample_proj(xs, g_attn_row, bf(jnp.pad(w, ((0, 0), (0, d_pad - d_in)))), SAMPLE_PROJ_TILE)
    parts, off = [], 0
    for n in _SPLITS[:-1]:
        parts.append(proj[:, off:off + n])
        off += n
    s_fq, s_fk, s_fv, s_ff, s_gq, s_gk, s_gv, s_ga, s_gr = parts
    lf_new = logsig_bias(s_ff, row(b_fox_f[0]))

    pool_T = lambda c: c.transpose(0, 2, 3, 1).reshape(c.shape[0], W_FOX, c.shape[1])
    decode_args = (_lane_rep(s_fq), _lane_rep(s_fk), _lane_rep(s_fv), _lane_rep(lf_new),
                   pool_T(cache_fox_k[0]), pool_T(cache_fox_v[0]), cache_fox_logf[0].transpose(0, 2, 1))

    xn, qT, ka, kT, vT, vT16, lfT, cT = fox_proj(
        x_prompt, g_attn_row, bfT(w_fq), bf(w_fk), bfT(w_fv), bfT(w_ff), col(b_fox_f[0]), PROMPT_TILE)
    fox_o = fox_attn(qT, cT, ka, vT16, PROMPT_TILE)
    gla_o, p_state = gla(xn, *ga_args, GLA_TILE)
    mkT, mvT = mem_kv(mem_prompt, row(g_mem[0]), bfT(w_mem_k[0]), bfT(w_mem_v[0]))
    mem_o = mem_attn(xn, bf(w_mq), mkT, mvT, PROMPT_TILE)
    N = B * S
    n_steps = N // ROW_TILE
    half = DB // 2
    n_pages = page_table.shape[1]
    plan_a = _plan_decode(0, half, n_steps, n_pages, DECODE_PAGES)
    plan_b = _plan_decode(half, DB - half, n_steps, n_pages, DECODE_PAGES)
    h, fox_o_a = merge(x_prompt.reshape(N, D), xn.reshape(N, D), fox_o.reshape(N, W_FOX),
                       gla_o.reshape(N, W_GLA_V), mem_o.reshape(N, W_MEM), *merge_w, ROW_TILE,
                       decode=(plan_a, page_table, decode_args))
    y_prompt, fox_o_b = mlp(h, *mlp_w, ROW_TILE, decode=(plan_b, page_table, decode_args))
    y_prompt = y_prompt.reshape(B, S, D)
    fox_o_s = jnp.concatenate([fox_o_a, fox_o_b], axis=0)

    heads_out = lambda aT, nh, hd: aT.reshape(aT.shape[0], nh, hd, aT.shape[2]).transpose(0, 3, 1, 2)[None]
    p_fox_k = heads_out(kT, H_FOX, HD_FOX)
    p_fox_v = heads_out(vT, H_FOX, HD_FOX)
    p_fox_logf = lfT.transpose(0, 2, 1)[None]
    p_mem_k = heads_out(mkT, H_MEM, HD_MEM)
    p_mem_v = heads_out(mvT, H_MEM, HD_MEM)

    gla_o_s, s_state = sample_gla(_lane_rep(s_gq), _lane_rep(s_gk), _lane_rep(s_ga), s_gv[:, None, :],
                                  s_gr[:, None, :], state_gla[0], bfT(w_gla_gate_up[0]), col(b_gla_gate[0]),
                                  row(g_gla_norm[0]))
    mem_T = lambda c: c.transpose(0, 2, 3, 1).reshape(c.shape[0], W_MEM, c.shape[1])
    xn_rows = jnp.broadcast_to(xn_s[:, None, :], (DB, SAMPLE_MEM_ROWS, D))
    mem_o_s = mem_attn(xn_rows, bf(w_mq), mem_T(cache_mem_k[0]), mem_T(cache_mem_v[0]), SAMPLE_MEM_ROWS)[:, 0]
    h_s, _ = merge(xs, xn_s, bf(fox_o_s.reshape(DB, W_FOX)), bf(gla_o_s.reshape(DB, W_GLA_V)), mem_o_s,
                   *merge_w, DB)
    y_sample, _ = mlp(h_s, *mlp_w, DB)
    y_sample = y_sample.reshape(DB, 1, D)

    s_fox_k = s_fk.reshape(1, DB, 1, H_FOX, HD_FOX)
    s_fox_v = s_fv.reshape(1, DB, 1, H_FOX, HD_FOX)
    s_fox_logf = lf_new.reshape(1, DB, 1, H_FOX)
    return (y_prompt, y_sample, p_fox_k, p_fox_v, p_fox_logf, p_state[None], p_mem_k, p_mem_v,
            s_fox_k, s_fox_v, s_fox_logf, s_state[None])
```
